```python
import jax, jax.numpy as jnp
from jax import lax
import numpy as np

D_MODEL = 1024
BATCH = 8
SEQ = 2048
DEPTH = 1
DEC_BATCH = 32
DEC_SEQ = 32
PAST_LEN = 4096

CHUNK = 64
N_META = 16
D_MIX = D_MODEL
D_HGRN = D_MIX // 2
D_POOL = D_MIX - D_HGRN
H_A = 4
DK = D_HGRN // H_A
DV = D_HGRN // H_A
POOL_WINDOWS = (2, 4, 8, 16)
N_POOL_GROUPS = 4
POOL_GW = D_POOL // N_POOL_GROUPS
POOL_HIST = 15
D_IN_PROJ = 4 * D_HGRN + D_POOL
PEER_HEADS = 8
N_KEYS = 128
N_EXPERTS = N_KEYS * N_KEYS
PEER_DKEY = 256
PEER_TOPK = 16
PEER_BLOCK = 128
EPS = 1e-6

kernel_name = 'hymba_hgrn2_pool_peer_stream_step'


def rmsnorm(x, w):
    xf = x.astype(jnp.float32)
    y = xf * lax.rsqrt(jnp.mean(xf * xf, axis=-1, keepdims=True) + EPS) * w.astype(jnp.float32)
    return y.astype(x.dtype)


def _hgrn2_chunk(S, blk):
    q, k, v, g = blk
    C = q.shape[2]
    b = jnp.cumsum(g, axis=2)
    o_inter = jnp.einsum('bhck,bhkv->bhcv', q * jnp.exp(b), S)
    causal = jnp.tril(jnp.ones((C, C), dtype=bool))
    diff = b[:, :, :, None, :] - b[:, :, None, :, :]
    decay = jnp.exp(jnp.where(causal[:, :, None], diff, -jnp.inf))
    A = jnp.einsum('bhtk,bhsk,bhtsk->bhts', q, k, decay)
    o = o_inter + jnp.einsum('bhts,bhsv->bhtv', A, v)
    b_last = b[:, :, -1, :]
    S_new = jnp.exp(b_last)[..., None] * S + jnp.einsum(
        'bhck,bhcv->bhkv', k * jnp.exp(b_last[:, :, None, :] - b), v)
    return S_new, o


def hgrn2_recurrence(q, k, v, g, S0):
    B, T = q.shape[0], q.shape[1]
    pad = (-T) % CHUNK

    def prep(a):
        a = jnp.pad(a, ((0, 0), (pad, 0), (0, 0), (0, 0)))
        n = a.shape[1] // CHUNK
        return a.reshape(B, n, CHUNK, H_A, a.shape[-1]).transpose(1, 0, 3, 2, 4)

    S, o = lax.scan(_hgrn2_chunk, S0, (prep(q), prep(k), prep(v), prep(g)))
    o = o.transpose(1, 0, 3, 2, 4).reshape(B, -1, H_A, DV)[:, pad:]
    return o, S


def pool_mixer(u_hist, u_new, n_hist, pool_w, pool_scale):
    T = u_new.shape[1]
    u_ext = jnp.concatenate([u_hist, u_new], axis=1)
    uf = u_ext.astype(jnp.float32)
    cs = jnp.pad(jnp.cumsum(uf, axis=1), ((0, 0), (1, 0), (0, 0)))
    avail = n_hist + jnp.arange(T, dtype=jnp.float32) + 1.0
    outs = []
    for gi, w in enumerate(POOL_WINDOWS):
        lo, hi = gi * POOL_GW, (gi + 1) * POOL_GW
        end = cs[:, POOL_HIST + 1:POOL_HIST + 1 + T, lo:hi]
        start = cs[:, POOL_HIST + 1 - w:POOL_HIST + 1 - w + T, lo:hi]
        cnt = jnp.minimum(jnp.float32(w), avail)[None, :, None]
        d = (end - start) / cnt - uf[:, POOL_HIST:, lo:hi]
        outs.append(jnp.einsum('btc,cd->btd', d, pool_w[gi].astype(jnp.float32)))
    out = jnp.concatenate(outs, axis=-1) * pool_scale.astype(jnp.float32)
    return out, u_ext[:, -POOL_HIST:]


def peer_ffn(x, w_query, subkeys, u_tab, v_tab):
    B, T, D = x.shape
    xt = x.reshape(-1, D)
    n = xt.shape[0]
    npad = (-n) % PEER_BLOCK
    blocks = jnp.pad(xt, ((0, npad), (0, 0))).reshape(-1, PEER_BLOCK, D)

    def block_fn(xb):
        q = (xb @ w_query).reshape(PEER_BLOCK, PEER_HEADS, 2, PEER_DKEY // 2)
        s = jnp.einsum('thpd,hpnd->thpn', q, subkeys)
        sv, si = lax.top_k(s, PEER_TOPK)
        cand = (sv[:, :, 0, :, None] + sv[:, :, 1, None, :]).reshape(PEER_BLOCK, PEER_HEADS, -1)
        cidx = (si[:, :, 0, :, None] * N_KEYS + si[:, :, 1, None, :]).reshape(PEER_BLOCK, PEER_HEADS, -1)
        top_s, pos = lax.top_k(cand, PEER_TOPK)
        experts = jnp.take_along_axis(cidx, pos, axis=-1)
        gate = jax.nn.softmax(top_s.astype(jnp.float32), axis=-1)
        u = jnp.take(u_tab, experts, axis=0)
        a = jax.nn.gelu(jnp.einsum('td,thkd->thk', xb, u).astype(jnp.float32), approximate=False)
        v = jnp.take(v_tab, experts, axis=0)
        return jnp.einsum('thk,thkd->td', (gate * a).astype(xb.dtype), v)

    ys = lax.map(block_fn, blocks)
    return ys.reshape(-1, D)[:n].reshape(B, T, D).astype(x.dtype)


def trunk_layer(h, S0, pool_hist, n_hist, lb, norm_mix, w_in, out_norm, pool_w, pool_scale,
                w_out, norm_ffn, peer_w_query, peer_subkeys, peer_u, peer_v):
    B, T, _ = h.shape
    xn = rmsnorm(h, norm_mix)
    proj = xn @ w_in
    zq, zf, zi, zg, zp = jnp.split(proj, [D_HGRN, 2 * D_HGRN, 3 * D_HGRN, 4 * D_HGRN], axis=-1)
    q = (jax.nn.silu(zq.astype(jnp.float32)) * DK ** -0.5).reshape(B, T, H_A, DK)
    zf = zf.astype(jnp.float32)
    log_f = jnp.log(lb + (1.0 - lb) * jax.nn.sigmoid(zf)).reshape(B, T, H_A, DK)
    k = ((1.0 - lb) * jax.nn.sigmoid(-zf)).reshape(B, T, H_A, DK)
    v = zi.astype(jnp.float32).reshape(B, T, H_A, DV)
    o, S_new = hgrn2_recurrence(q, k, v, log_f, S0.astype(jnp.float32))
    o = o * lax.rsqrt(jnp.mean(o * o, axis=-1, keepdims=True) + EPS)
    o = o.reshape(B, T, D_HGRN) * out_norm.astype(jnp.float32) * jax.nn.silu(zg.astype(jnp.float32))
    pool_out, new_hist = pool_mixer(pool_hist.astype(zp.dtype), zp, n_hist, pool_w, pool_scale)
    mix = jnp.concatenate([o.astype(h.dtype), pool_out.astype(h.dtype)], axis=-1) @ w_out
    h = h + mix
    h = h + peer_ffn(rmsnorm(h, norm_ffn), peer_w_query, peer_subkeys, peer_u, peer_v)
    return h, S_new.astype(h.dtype), new_hist


def setup_inputs(seed: int = 0) -> dict:
    key = jax.random.key(seed)
    ks = jax.random.split(key, 18)
    f32 = jnp.float32

    def nrm(k, shape, scale):
        return jax.random.normal(k, shape, f32) * scale

    return {
        'x_prompt': nrm(ks[0], (BATCH, SEQ, D_MODEL), 1.0),
        'x_sample': nrm(ks[1], (DEC_BATCH, DEC_SEQ, D_MODEL), 1.0),
        'state_hgrn': nrm(ks[2], (DEPTH, DEC_BATCH, H_A, DK, DV), 0.5),
        'cache_pool': nrm(ks[3], (DEPTH, DEC_BATCH, POOL_HIST, D_POOL), 1.0),
        'meta_tokens': nrm(ks[4], (N_META, D_MODEL), 1.0),
        'norm_mix': 1.0 + nrm(ks[5], (DEPTH, D_MODEL), 0.02),
        'w_in': nrm(ks[6], (DEPTH, D_MODEL, D_IN_PROJ), D_MODEL ** -0.5),
        'hgrn_lb_logits': nrm(ks[7], (DEPTH + 1, D_HGRN), 0.5),
        'hgrn_out_norm': 1.0 + nrm(ks[8], (DEPTH, D_HGRN), 0.02),
        'pool_w': nrm(ks[9], (DEPTH, N_POOL_GROUPS, POOL_GW, POOL_GW), POOL_GW ** -0.5),
        'pool_scale': 0.5 + nrm(ks[10], (DEPTH, D_POOL), 0.05),
        'w_out': nrm(ks[11], (DEPTH, D_MIX, D_MODEL), D_MIX ** -0.5),
        'norm_ffn': 1.0 + nrm(ks[12], (DEPTH, D_MODEL), 0.02),
        'peer_w_query': nrm(ks[13], (DEPTH, D_MODEL, PEER_HEADS * PEER_DKEY), D_MODEL ** -0.5),
        'peer_subkeys': nrm(ks[14], (DEPTH, PEER_HEADS, 2, N_KEYS, PEER_DKEY // 2), (PEER_DKEY // 2) ** -0.5),
        'peer_u': nrm(ks[15], (DEPTH, N_EXPERTS, D_MODEL), D_MODEL ** -0.5),
        'peer_v': nrm(ks[16], (DEPTH, N_EXPERTS, D_MODEL), PEER_HEADS ** -0.5),
        'norm_final': 1.0 + nrm(ks[17], (D_MODEL,), 0.02),
    }


def reference(x_prompt, x_sample, state_hgrn, cache_pool, meta_tokens, norm_mix, w_in,
              hgrn_lb_logits, hgrn_out_norm, pool_w, pool_scale, w_out, norm_ffn,
              peer_w_query, peer_subkeys, peer_u, peer_v, norm_final):
    lb_all = jnp.cumsum(jax.nn.softmax(hgrn_lb_logits.astype(jnp.float32), axis=0), axis=0)
    Bp = x_prompt.shape[0]
    meta = jnp.broadcast_to(meta_tokens[None].astype(x_prompt.dtype), (Bp, N_META, D_MODEL))
    hp = jnp.concatenate([meta, x_prompt], axis=1)
    hs = x_sample
    S_p0 = jnp.zeros((Bp, H_A, DK, DV), jnp.float32)
    pool_p0 = jnp.zeros((Bp, POOL_HIST, D_POOL), x_prompt.dtype)
    sp_list, cp_list, ss_list, cs_list = [], [], [], []
    for l in range(DEPTH):
        w = (lb_all[l], norm_mix[l], w_in[l], hgrn_out_norm[l], pool_w[l], pool_scale[l],
             w_out[l], norm_ffn[l], peer_w_query[l], peer_subkeys[l], peer_u[l], peer_v[l])
        hp, sp, cp = trunk_layer(hp, S_p0, pool_p0, 0, *w)
        hs, ss, cs = trunk_layer(hs, state_hgrn[l], cache_pool[l], POOL_HIST, *w)
        sp_list.append(sp)
        cp_list.append(cp)
        ss_list.append(ss)
        cs_list.append(cs)
    y_prompt = rmsnorm(hp[:, N_META:], norm_final)
    y_sample = rmsnorm(hs, norm_final)
    state_hgrn_prompt = jnp.stack(sp_list, axis=0)
    cache_pool_prompt = jnp.stack(cp_list, axis=0)
    state_hgrn_sample = jnp.stack(ss_list, axis=0)
    cache_pool_sample = jnp.stack(cs_list, axis=0)
    return (y_prompt, y_sample, state_hgrn_prompt, cache_pool_prompt, state_hgrn_sample, cache_pool_sample)
```

```python
import functools

import jax
import jax.numpy as jnp
from jax import lax
from jax.experimental import pallas as pl
from jax.experimental.pallas import tpu as pltpu

F32 = jnp.float32
BF16 = jnp.bfloat16

D_MODEL = 1024
N_META = 16
D_HGRN = 512
D_POOL = 512
H_A = 4
DK = 128
DV = 128
POOL_WINDOWS = (2, 4, 8, 16)
POOL_GW = 128
POOL_HIST = 15
D_IN_PROJ = 4 * D_HGRN + D_POOL
PEER_HEADS = 8
N_KEYS = 128
N_EXPERTS = N_KEYS * N_KEYS
PEER_TOPK = 16
EPS = 1e-6
CHUNK = 64
HIST_ROWS = 16
BASE = 8

VMEM_LIMIT_BYTES = 48 * 1024 * 1024

TM_PROJ = 256
TB_TOPK = 128
TB_PEER = 64
N_SLOTS = 2
HITS = PEER_HEADS * PEER_TOPK
NEG_BIG = -1e30


def _cparams(sem):
    return pltpu.CompilerParams(dimension_semantics=sem, vmem_limit_bytes=VMEM_LIMIT_BYTES)


def _inproj_kernel(h_ref, nw_ref, win_ref, lbl_ref, q_ref, k_ref, lf_ref, v_ref, g_ref, zp_ref):
    x = h_ref[...]
    ms = jnp.mean(x * x, axis=-1, keepdims=True)
    xn = x * lax.rsqrt(ms + EPS) * nw_ref[...]
    proj = jnp.dot(xn.astype(BF16), win_ref[...], preferred_element_type=F32)
    zq = proj[:, 0:D_HGRN]
    zf = proj[:, D_HGRN:2 * D_HGRN]
    zi = proj[:, 2 * D_HGRN:3 * D_HGRN]
    zg = proj[:, 3 * D_HGRN:4 * D_HGRN]
    zp = proj[:, 4 * D_HGRN:]
    lg = lbl_ref[...]
    e = jnp.exp(lg - jnp.max(lg, axis=0, keepdims=True))
    lb = e[0:1, :] / jnp.sum(e, axis=0, keepdims=True)
    q_ref[...] = jax.nn.silu(zq) * (DK ** -0.5)
    lf_ref[...] = jnp.log(lb + (1.0 - lb) * jax.nn.sigmoid(zf))
    k_ref[...] = (1.0 - lb) * jax.nn.sigmoid(-zf)
    v_ref[...] = zi
    g_ref[...] = jax.nn.silu(zg)
    zp_ref[...] = zp


def _inproj(h2d, norm_w, w_in_bf, lb_logits):
    n = h2d.shape[0]
    tm = TM_PROJ
    assert n % tm == 0
    row = lambda i: (i, 0)
    const = lambda i: (0, 0)
    out = jax.ShapeDtypeStruct((n, D_HGRN), F32)
    return pl.pallas_call(
        _inproj_kernel,
        grid=(n // tm,),
        in_specs=[
            pl.BlockSpec((tm, D_MODEL), row),
            pl.BlockSpec((1, D_MODEL), const),
            pl.BlockSpec((D_MODEL, D_IN_PROJ), const),
            pl.BlockSpec(lb_logits.shape, const),
        ],
        out_specs=[pl.BlockSpec((tm, D_HGRN), row)] * 6,
        out_shape=[out] * 6,
        compiler_params=_cparams(("parallel",)),
        name="inproj",
    )(h2d, norm_w.reshape(1, D_MODEL), w_in_bf, lb_logits)


def _levels(c):
    out, m = [], BASE
    while m < c:
        out.append(m)
        m *= 2
    return out


def _blk(x, m):
    return jnp.right_shift(x, m.bit_length() - 1)


def _decay_sum_matrix(c):
    t = lax.broadcasted_iota(jnp.int32, (c, c), 0)
    r = lax.broadcasted_iota(jnp.int32, (c, c), 1)
    mats = [r <= t]
    for m in _levels(c):
        lo = _blk(t, m) * m
        mats.append((r >= lo) & (r <= t))
        mats.append((r > t) & (r <= lo + m - 1))
    return jnp.concatenate([x.astype(F32) for x in mats], axis=0).astype(BF16)


def _mixer_kernel(q_ref, k_ref, lf_ref, v_ref, g_ref, zp_ref, s0_ref, hist_ref, onorm_ref,
                  pw_ref, ps_ref, mix_ref, sout_ref, st_scr, prev_scr, *, c, pad, n_hist):
    ci = pl.program_id(1)
    nc = pl.num_programs(1)

    @pl.when(ci == 0)
    def _():
        for hd in range(H_A):
            st_scr[hd] = s0_ref[0, hd].T
        prev_scr[...] = hist_ref[0]

    row = ci * c + lax.broadcasted_iota(jnp.int32, (c, 1), 0)
    live = row >= pad
    lf_all = jnp.where(live, lf_ref[...], 0.0)
    k_all = jnp.where(live, k_ref[...], 0.0)
    q_all = q_ref[...]
    v_all = v_ref[...]

    wmat = _decay_sum_matrix(c)
    lf_hi = lf_all.astype(BF16)
    r1 = lf_all - lf_hi.astype(F32)
    lf_mid = r1.astype(BF16)
    lf_lo = (r1 - lf_mid.astype(F32)).astype(BF16)
    sums = (jnp.dot(wmat, lf_hi, preferred_element_type=F32)
            + jnp.dot(wmat, lf_mid, preferred_element_type=F32)
            + jnp.dot(wmat, lf_lo, preferred_element_type=F32))

    levels = _levels(c)
    t_i = lax.broadcasted_iota(jnp.int32, (c, c), 0)
    s_i = lax.broadcasted_iota(jnp.int32, (c, c), 1)
    row_i = lax.broadcasted_iota(jnp.int32, (c, 1), 0)
    sub_i = lax.broadcasted_iota(jnp.int32, (c // BASE, BASE, 1), 1)
    ones_b = jnp.ones((DK, DV), BF16)
    onorm = onorm_ref[...]
    g_all = g_ref[...]

    o_heads = []
    for hd in range(H_A):
        sl = slice(hd * DK, (hd + 1) * DK)
        q = q_all[:, sl]
        k = k_all[:, sl]
        v = v_all[:, sl]
        b = sums[0:c, sl]
        st = st_scr[hd]

        o = lax.dot_general((q * jnp.exp(b)).astype(BF16), st.astype(BF16),
                            (((1,), (1,)), ((), ())), preferred_element_type=F32)

        a_mat = jnp.zeros((c, c), F32)
        for li, m in enumerate(levels):
            eq = sums[(1 + 2 * li) * c:(2 + 2 * li) * c, sl]
            ek = sums[(2 + 2 * li) * c:(3 + 2 * li) * c, sl]
            odd = (_blk(row_i, m) & 1) == 1
            qs = q * jnp.exp(jnp.where(odd, eq, NEG_BIG))
            ks = k * jnp.exp(jnp.where(odd, NEG_BIG, ek))
            am = lax.dot_general(qs.astype(BF16), ks.astype(BF16),
                                 (((1,), (1,)), ((), ())), preferred_element_type=F32)
            pair = ((_blk(t_i, 2 * m) == _blk(s_i, 2 * m))
                    & ((_blk(t_i, m) & 1) == 1) & ((_blk(s_i, m) & 1) == 0))
            a_mat = a_mat + jnp.where(pair, am, 0.0)
        o = o + jnp.dot(a_mat.astype(BF16), v.astype(BF16), preferred_element_type=F32)

        q3 = q.reshape(c // BASE, BASE, DK)
        k3 = k.reshape(c // BASE, BASE, DK)
        v3 = v.reshape(c // BASE, BASE, DV)
        b3 = b.reshape(c // BASE, BASE, DK)
        prods = []
        for d in range(BASE):
            if d == 0:
                prods.append(q3 * k3)
            else:
                kd = pltpu.roll(k3, d, 1)
                bd = pltpu.roll(b3, d, 1)
                dec = jnp.exp(jnp.where(sub_i >= d, b3 - bd, NEG_BIG))
                prods.append(q3 * kd * dec)
        pst = jnp.concatenate(prods, axis=0).reshape(BASE * c, DK)
        asum = jnp.dot(pst.astype(BF16), ones_b, preferred_element_type=F32)
        for d in range(BASE):
            vd = v if d == 0 else pltpu.roll(v3, d, 1).reshape(c, DV)
            o = o + asum[d * c:(d + 1) * c, :] * vd

        bl = b[c - 1:c, :]
        kdec = k * jnp.exp(bl - b)
        upd = lax.dot_general(v.astype(BF16), kdec.astype(BF16),
                              (((0,), (0,)), ((), ())), preferred_element_type=F32)
        st_new = st * jnp.exp(bl) + upd
        st_scr[hd] = st_new

        @pl.when(ci == nc - 1)
        def _():
            sout_ref[0, hd] = st_new.T

        o = o * lax.rsqrt(jnp.mean(o * o, axis=-1, keepdims=True) + EPS)
        o_heads.append(o * onorm[:, sl] * g_all[:, sl])

    zp = zp_ref[...]
    ext = jnp.concatenate([prev_scr[...], zp], axis=0)
    tok = (row - pad).astype(F32)
    avail = jnp.maximum(n_hist + tok + 1.0, 1.0)
    pool_heads = []
    for gi, w in enumerate(POOL_WINDOWS):
        sl = slice(gi * POOL_GW, (gi + 1) * POOL_GW)
        s = ext[:, sl]
        sh = 1
        while sh < w:
            s = s + pltpu.roll(s, sh, 0)
            sh *= 2
        win = s[HIST_ROWS:, :]
        cnt = jnp.minimum(float(w), avail)
        dlt = win / cnt - zp[:, sl]
        po = jnp.dot(dlt.astype(BF16), pw_ref[gi], preferred_element_type=F32)
        pool_heads.append(po * ps_ref[:, sl])
    prev_scr[...] = zp[c - HIST_ROWS:, :]

    mix_ref[...] = jnp.concatenate(o_heads + pool_heads, axis=1).astype(BF16)


def _mixer(acts, s0, hist, out_norm, pool_w_bf, pool_scale, *, batch, t_len, c, pad, n_hist, skip):
    nc = t_len // c
    assert nc * c == t_len and c % BASE == 0 and c >= HIST_ROWS
    blk = lambda b, ci: (b * nc + ci, 0)
    per_b4 = lambda b, ci: (b, 0, 0, 0)
    per_b3 = lambda b, ci: (b, 0, 0)
    c2 = lambda b, ci: (0, 0)
    c3 = lambda b, ci: (0, 0, 0)
    n_main = nc - skip
    mix_map = lambda b, ci: (b * n_main + jnp.maximum(ci - skip, 0), 0)
    kern = functools.partial(_mixer_kernel, c=c, pad=pad, n_hist=float(n_hist))
    return pl.pallas_call(
        kern,
        grid=(batch, nc),
        in_specs=[pl.BlockSpec((c, D_HGRN), blk)] * 6 + [
            pl.BlockSpec((1, H_A, DK, DV), per_b4),
            pl.BlockSpec((1, HIST_ROWS, D_POOL), per_b3),
            pl.BlockSpec((1, D_HGRN), c2),
            pl.BlockSpec((len(POOL_WINDOWS), POOL_GW, POOL_GW), c3),
            pl.BlockSpec((1, D_POOL), c2),
        ],
        out_specs=[
            pl.BlockSpec((c, D_MODEL), mix_map),
            pl.BlockSpec((1, H_A, DK, DV), per_b4),
        ],
        out_shape=[
            jax.ShapeDtypeStruct((batch * n_main * c, D_MODEL), BF16),
            jax.ShapeDtypeStruct((batch, H_A, DK, DV), F32),
        ],
        scratch_shapes=[
            pltpu.VMEM((H_A, DV, DK), F32),
            pltpu.VMEM((HIST_ROWS, D_POOL), F32),
        ],
        compiler_params=_cparams(("parallel", "arbitrary")),
        name="mixer",
    )(*acts, s0, hist, out_norm.reshape(1, D_HGRN), pool_w_bf, pool_scale.reshape(1, D_POOL))


def _outproj_kernel(h_ref, mix_ref, wout_ref, nw_ref, h1_ref, xn_ref):
    h1 = h_ref[...] + jnp.dot(mix_ref[...], wout_ref[...], preferred_element_type=F32)
    h1_ref[...] = h1
    ms = jnp.mean(h1 * h1, axis=-1, keepdims=True)
    xn_ref[...] = h1 * lax.rsqrt(ms + EPS) * nw_ref[...]


def _outproj(h2d, mix, w_out_bf, norm_w):
    n = h2d.shape[0]
    tm = TM_PROJ
    assert n % tm == 0
    row = lambda i: (i, 0)
    const = lambda i: (0, 0)
    out = jax.ShapeDtypeStruct((n, D_MODEL), F32)
    return pl.pallas_call(
        _outproj_kernel,
        grid=(n // tm,),
        in_specs=[
            pl.BlockSpec((tm, D_MODEL), row),
            pl.BlockSpec((tm, D_MODEL), row),
            pl.BlockSpec((D_MODEL, D_MODEL), const),
            pl.BlockSpec((1, D_MODEL), const),
        ],
        out_specs=[pl.BlockSpec((tm, D_MODEL), row)] * 2,
        out_shape=[out, out],
        compiler_params=_cparams(("parallel",)),
        name="outproj",
    )(h2d, mix, w_out_bf, norm_w.reshape(1, D_MODEL))


def _keymat_kernel(sk_ref, wq_ref, m_ref):
    m = lax.dot_general(sk_ref[0], wq_ref[...], (((1,), (1,)), ((), ())),
                        precision=lax.Precision.HIGHEST, preferred_element_type=F32)
    m_ref[...] = m.astype(BF16)


def _keymat(subkeys, w_query):
    nhp = PEER_HEADS * 2
    sk = subkeys.reshape(nhp, N_KEYS, N_KEYS)
    return pl.pallas_call(
        _keymat_kernel,
        grid=(nhp,),
        in_specs=[
            pl.BlockSpec((1, N_KEYS, N_KEYS), lambda i: (i, 0, 0)),
            pl.BlockSpec((D_MODEL, N_KEYS), lambda i: (0, i)),
        ],
        out_specs=pl.BlockSpec((N_KEYS, D_MODEL), lambda i: (i, 0)),
        out_shape=jax.ShapeDtypeStruct((nhp * N_KEYS, D_MODEL), BF16),
        compiler_params=_cparams(("parallel",)),
        name="keymat",
    )(sk, w_query)


_STAIR = [(i, j) for i in range(PEER_TOPK) for j in range(PEER_TOPK) if (i + 1) * (j + 1) <= PEER_TOPK]


def _topk_kernel(xn_ref, m_ref, exp_ref, gate_ref, sc_scr, sv_scr, si_scr):
    tb = xn_ref.shape[0]
    xb = xn_ref[...].astype(BF16)
    sc_scr[...] = lax.dot_general(m_ref[...], xb, (((1,), (1,)), ((), ())), preferred_element_type=F32)
    key_i = lax.broadcasted_iota(jnp.int32, (N_KEYS, tb), 0).astype(F32)

    def hp_body(hp, carry):
        s = sc_scr[pl.ds(pl.multiple_of(hp * N_KEYS, N_KEYS), N_KEYS), :]
        h = hp // 2
        p = hp % 2
        for i in range(PEER_TOPK):
            mx = jnp.max(s, axis=0, keepdims=True)
            idx = jnp.min(jnp.where(s == mx, key_i, float(N_KEYS)), axis=0, keepdims=True)
            s = jnp.where(key_i == idx, -jnp.inf, s)
            sv_scr[p, i, pl.ds(h, 1), :] = mx
            si_scr[p, i, pl.ds(h, 1), :] = idx
        return carry

    lax.fori_loop(0, 2 * PEER_HEADS, hp_body, 0)

    sv0 = [sv_scr[0, i] for i in range(PEER_TOPK)]
    sv1 = [sv_scr[1, i] for i in range(PEER_TOPK)]
    si0 = [si_scr[0, i] for i in range(PEER_TOPK)]
    si1 = [si_scr[1, i] for i in range(PEER_TOPK)]
    cands = [sv0[i] + sv1[j] for (i, j) in _STAIR]
    flats = [float(i * PEER_TOPK + j) for (i, j) in _STAIR]
    tops, experts = [], []
    for r in range(PEER_TOPK):
        mx = functools.reduce(jnp.maximum, cands)
        sel = functools.reduce(jnp.minimum, [jnp.where(cv == mx, fl, 1e9) for cv, fl in zip(cands, flats)])
        cands = [jnp.where(sel == fl, -jnp.inf, cv) for cv, fl in zip(cands, flats)]
        fi = jnp.floor(sel * (1.0 / PEER_TOPK))
        fj = sel - fi * PEER_TOPK
        e0 = functools.reduce(jnp.add, [jnp.where(fi == float(i), si0[i], 0.0) for i in range(PEER_TOPK)])
        e1 = functools.reduce(jnp.add, [jnp.where(fj == float(j), si1[j], 0.0) for j in range(PEER_TOPK)])
        tops.append(mx)
        experts.append((e0 * N_KEYS + e1).astype(jnp.int32))
    ex = [jnp.exp(t - tops[0]) for t in tops]
    den = functools.reduce(jnp.add, ex)
    for r in range(PEER_TOPK):
        exp_ref[0, r] = experts[r]
        gate_ref[0, r] = ex[r] / den


def _topk(xn, keymat):
    n = xn.shape[0]
    tb = TB_TOPK
    assert n % tb == 0
    nb = n // tb
    shp = (nb, PEER_TOPK, PEER_HEADS, tb)
    ospec = pl.BlockSpec((1, PEER_TOPK, PEER_HEADS, tb), lambda i: (i, 0, 0, 0))
    return pl.pallas_call(
        _topk_kernel,
        grid=(nb,),
        in_specs=[
            pl.BlockSpec((tb, D_MODEL), lambda i: (i, 0)),
            pl.BlockSpec(keymat.shape, lambda i: (0, 0)),
        ],
        out_specs=[ospec, ospec],
        out_shape=[jax.ShapeDtypeStruct(shp, jnp.int32), jax.ShapeDtypeStruct(shp, F32)],
        scratch_shapes=[
            pltpu.VMEM((2 * PEER_HEADS * N_KEYS, tb), F32),
            pltpu.VMEM((2, PEER_TOPK, PEER_HEADS, tb), F32),
            pltpu.VMEM((2, PEER_TOPK, PEER_HEADS, tb), F32),
        ],
        compiler_params=_cparams(("parallel",)),
        name="topk",
    )(xn, keymat)


ROW_TILES = 2 * D_MODEL // 128
HALF_TILES = D_MODEL // 128


SUB = 8


def _as_tile(row):
    return jnp.concatenate([row[:, c * 128:(c + 1) * 128] for c in range(HALF_TILES)], axis=0)


def _as_row(tile):
    return jnp.concatenate([tile[c:c + 1, :] for c in range(HALF_TILES)], axis=1)


def _peer_kernel(idx_ref, gate_ref, xn_ref, h1_ref, nf_ref, gsum_ref, tab_ref, y_ref,
                 buf, w_scr, sem):
    tb = xn_ref.shape[0]

    def gather(j, slot):
        for e in range(HITS):
            pltpu.make_async_copy(tab_ref.at[idx_ref[j, e]], buf.at[slot, e], sem.at[slot]).start()

    def gather_wait(slot):
        pltpu.make_async_copy(tab_ref.at[pl.ds(0, HITS)], buf.at[slot], sem.at[slot]).wait()

    eye = (lax.broadcasted_iota(jnp.int32, (HITS, HITS), 0)
           == lax.broadcasted_iota(jnp.int32, (HITS, HITS), 1)).astype(F32)
    nf = _as_tile(nf_ref[...])
    sub = lax.broadcasted_iota(jnp.int32, (SUB, 1), 0)

    gather(0, 0)

    def group_body(g, carry):
        base = pl.multiple_of(g * SUB, SUB)
        x8 = xn_ref[pl.ds(base, SUB), :]
        h8 = h1_ref[pl.ds(base, SUB), :]
        g8 = gate_ref[pl.ds(base, SUB), :]

        def tok_body(i, y8):
            j = base + i
            slot = j % N_SLOTS

            @pl.when(j + 1 < tb)
            def _():
                gather(j + 1, (j + 1) % N_SLOTS)

            gather_wait(slot)
            mine = sub == i
            pick = lambda a: jnp.sum(jnp.where(mine, a, 0.0), axis=0, keepdims=True)
            x3 = _as_tile(pick(x8))
            u3 = buf[slot, :, 0:HALF_TILES, :]
            prod = (u3 * x3[None]).reshape(HITS * HALF_TILES, 128)
            part = jnp.dot(gsum_ref[...], prod.astype(BF16), preferred_element_type=F32)
            a_col = jnp.sum(part, axis=1, keepdims=True)
            g_col = jnp.sum(eye * pick(g8), axis=1, keepdims=True)
            w_col = g_col * (0.5 * a_col * (1.0 + lax.erf(a_col * (2.0 ** -0.5))))
            w_scr[...] = jnp.broadcast_to(w_col, (HITS, 128))
            accs = [jnp.zeros((HALF_TILES, 128), F32) for _ in range(4)]
            for e in range(HITS):
                accs[e % 4] = accs[e % 4] + w_scr[pl.ds(e, 1), :] * buf[slot, e, HALF_TILES:ROW_TILES, :]
            y3 = _as_tile(pick(h8)) + ((accs[0] + accs[1]) + (accs[2] + accs[3]))
            ms = jnp.sum(jnp.sum(y3 * y3, axis=1, keepdims=True), axis=0, keepdims=True) * (1.0 / D_MODEL)
            y3 = y3 * lax.rsqrt(ms + EPS) * nf
            return jnp.where(mine, _as_row(y3), y8)

        y_ref[pl.ds(base, SUB), :] = lax.fori_loop(0, SUB, tok_body, jnp.zeros((SUB, D_MODEL), F32))
        return carry

    lax.fori_loop(0, tb // SUB, group_body, 0)


def _peer(idx, gates, xn, h1, norm_final, gsum, table):
    n = xn.shape[0]
    tb = TB_PEER
    assert n % tb == 0
    row = lambda i: (i, 0)
    const = lambda i: (0, 0)
    return pl.pallas_call(
        _peer_kernel,
        grid=(n // tb,),
        in_specs=[
            pl.BlockSpec((tb, HITS), row, memory_space=pltpu.SMEM),
            pl.BlockSpec((tb, HITS), row),
            pl.BlockSpec((tb, D_MODEL), row),
            pl.BlockSpec((tb, D_MODEL), row),
            pl.BlockSpec((1, D_MODEL), const),
            pl.BlockSpec(gsum.shape, const),
            pl.BlockSpec(memory_space=pl.ANY),
        ],
        out_specs=pl.BlockSpec((tb, D_MODEL), row),
        out_shape=jax.ShapeDtypeStruct((n, D_MODEL), F32),
        scratch_shapes=[
            pltpu.VMEM((N_SLOTS, HITS, ROW_TILES, 128), F32),
            pltpu.VMEM((HITS, 128), F32),
            pltpu.SemaphoreType.DMA((N_SLOTS,)),
        ],
        compiler_params=_cparams(("arbitrary",)),
        name="peer",
    )(idx, gates, xn, h1, norm_final.reshape(1, D_MODEL), gsum, table)


def _token_major(a):
    nb, r, h, t = a.shape
    return jnp.transpose(a, (0, 3, 2, 1)).reshape(nb * t, h * r)


def _stream(h_pad, h_main, s0, hist, n_hist, pad, skip, c, consts):
    (norm_mix, w_in_bf, lb_logits, out_norm, pool_w_bf, pool_scale, w_out_bf, norm_ffn,
     keymat, gsum, table, norm_final) = consts
    batch, t_len, _ = h_pad.shape
    acts = _inproj(h_pad.reshape(batch * t_len, D_MODEL), norm_mix, w_in_bf, lb_logits)
    mix, s_new = _mixer(acts, s0, hist, out_norm, pool_w_bf, pool_scale,
                        batch=batch, t_len=t_len, c=c, pad=pad, n_hist=n_hist, skip=skip)
    zp = acts[5].reshape(batch, t_len, D_POOL)
    new_hist = zp[:, t_len - POOL_HIST:, :]
    hm = h_main.reshape(-1, D_MODEL)
    h1, xn = _outproj(hm, mix, w_out_bf, norm_ffn)
    experts, gates = _topk(xn, keymat)
    y = _peer(_token_major(experts), _token_major(gates), xn, h1, norm_final, gsum, table)
    return y.reshape(h_main.shape), s_new, new_hist


def kernel(x_prompt, x_sample, state_hgrn, cache_pool, meta_tokens, norm_mix, w_in, hgrn_lb_logits,
           hgrn_out_norm, pool_w, pool_scale, w_out, norm_ffn, peer_w_query, peer_subkeys, peer_u,
           peer_v, norm_final):
    assert state_hgrn.shape[0] == 1, "single trunk layer"
    bp, seq, _ = x_prompt.shape
    bs, dseq, _ = x_sample.shape
    dt = x_prompt.dtype

    keymat = _keymat(peer_subkeys[0], peer_w_query[0])
    table = jnp.concatenate([peer_u[0], peer_v[0]], axis=1).reshape(N_EXPERTS, ROW_TILES, 128)
    r = jnp.arange(HITS * HALF_TILES, dtype=jnp.int32)[None, :] // HALF_TILES
    gsum = (r == jnp.arange(HITS, dtype=jnp.int32)[:, None]).astype(BF16)
    consts = (norm_mix[0], w_in[0].astype(BF16), hgrn_lb_logits, hgrn_out_norm[0],
              pool_w[0].astype(BF16), pool_scale[0], w_out[0].astype(BF16), norm_ffn[0],
              keymat, gsum, table, norm_final)

    pad = CHUNK - N_META
    head = jnp.concatenate([jnp.zeros((bp, pad, D_MODEL), dt),
                            jnp.broadcast_to(meta_tokens[None].astype(dt), (bp, N_META, D_MODEL))], axis=1)
    hp = jnp.concatenate([head, x_prompt], axis=1)
    s_p0 = jnp.zeros((bp, H_A, DK, DV), F32)
    hist_p0 = jnp.zeros((bp, HIST_ROWS, D_POOL), dt)
    y_p, s_p, c_p = _stream(hp, x_prompt, s_p0, hist_p0, 0, pad, 1, CHUNK, consts)

    hist_s0 = jnp.concatenate([jnp.zeros((bs, HIST_ROWS - POOL_HIST, D_POOL), dt), cache_pool[0]], axis=1)
    y_s, s_s, c_s = _stream(x_sample, x_sample, state_hgrn[0], hist_s0, POOL_HIST, 0, 0, dseq, consts)

    return (y_p, y_s, s_p[None].astype(dt), c_p[None], s_s[None].astype(dt), c_s[None])
```

```python
import functools

import jax
import jax.numpy as jnp
from jax import lax
from jax.experimental import pallas as pl
from jax.experimental.pallas import tpu as pltpu

F32 = jnp.float32
BF16 = jnp.bfloat16

D_MODEL = 1024
N_META = 16
D_HGRN = 512
D_POOL = 512
H_A = 4
DK = 128
DV = 128
POOL_WINDOWS = (2, 4, 8, 16)
POOL_GW = 128
POOL_HIST = 15
D_IN_PROJ = 4 * D_HGRN + D_POOL
PEER_HEADS = 8
N_KEYS = 128
N_EXPERTS = N_KEYS * N_KEYS
PEER_TOPK = 16
EPS = 1e-6
CHUNK = 64
HIST_ROWS = 16
BASE = 8

VMEM_LIMIT_BYTES = 48 * 1024 * 1024

TM_PROJ = 256
TB_TOPK = 128
TB_PEER = 64
N_SLOTS = 8
AHEAD = 6
HITS = PEER_HEADS * PEER_TOPK
NEG_BIG = -1e30


def _cparams(sem):
    return pltpu.CompilerParams(dimension_semantics=sem, vmem_limit_bytes=VMEM_LIMIT_BYTES)


def _inproj_kernel(h_ref, nw_ref, win_ref, lbl_ref, q_ref, k_ref, lf_ref, v_ref, g_ref, zp_ref):
    x = h_ref[...]
    ms = jnp.mean(x * x, axis=-1, keepdims=True)
    xn = x * lax.rsqrt(ms + EPS) * nw_ref[...]
    proj = jnp.dot(xn.astype(BF16), win_ref[...], preferred_element_type=F32)
    zq = proj[:, 0:D_HGRN]
    zf = proj[:, D_HGRN:2 * D_HGRN]
    zi = proj[:, 2 * D_HGRN:3 * D_HGRN]
    zg = proj[:, 3 * D_HGRN:4 * D_HGRN]
    zp = proj[:, 4 * D_HGRN:]
    lg = lbl_ref[...]
    e = jnp.exp(lg - jnp.max(lg, axis=0, keepdims=True))
    lb = e[0:1, :] / jnp.sum(e, axis=0, keepdims=True)
    q_ref[...] = jax.nn.silu(zq) * (DK ** -0.5)
    lf_ref[...] = jnp.log(lb + (1.0 - lb) * jax.nn.sigmoid(zf))
    k_ref[...] = (1.0 - lb) * jax.nn.sigmoid(-zf)
    v_ref[...] = zi
    g_ref[...] = jax.nn.silu(zg)
    zp_ref[...] = zp


def _inproj(h2d, norm_w, w_in_bf, lb_logits):
    n = h2d.shape[0]
    tm = TM_PROJ
    assert n % tm == 0
    row = lambda i: (i, 0)
    const = lambda i: (0, 0)
    out = jax.ShapeDtypeStruct((n, D_HGRN), F32)
    return pl.pallas_call(
        _inproj_kernel,
        grid=(n // tm,),
        in_specs=[
            pl.BlockSpec((tm, D_MODEL), row),
            pl.BlockSpec((1, D_MODEL), const),
            pl.BlockSpec((D_MODEL, D_IN_PROJ), const),
            pl.BlockSpec(lb_logits.shape, const),
        ],
        out_specs=[pl.BlockSpec((tm, D_HGRN), row)] * 6,
        out_shape=[out] * 6,
        compiler_params=_cparams(("parallel",)),
        name="inproj",
    )(h2d, norm_w.reshape(1, D_MODEL), w_in_bf, lb_logits)


def _levels(c):
    out, m = [], BASE
    while m < c:
        out.append(m)
        m *= 2
    return out


def _blk(x, m):
    return jnp.right_shift(x, m.bit_length() - 1)


def _decay_sum_matrix(c):
    t = lax.broadcasted_iota(jnp.int32, (c, c), 0)
    r = lax.broadcasted_iota(jnp.int32, (c, c), 1)
    mats = [r <= t]
    for m in _levels(c):
        lo = _blk(t, m) * m
        mats.append((r >= lo) & (r <= t))
        mats.append((r > t) & (r <= lo + m - 1))
    return jnp.concatenate([x.astype(F32) for x in mats], axis=0).astype(BF16)


def _mixer_kernel(q_ref, k_ref, lf_ref, v_ref, g_ref, zp_ref, s0_ref, hist_ref, onorm_ref,
                  pw_ref, ps_ref, mix_ref, sout_ref, st_scr, prev_scr, *, c, pad, n_hist):
    ci = pl.program_id(1)
    nc = pl.num_programs(1)

    @pl.when(ci == 0)
    def _():
        for hd in range(H_A):
            st_scr[hd] = s0_ref[0, hd].T
        prev_scr[...] = hist_ref[0]

    row = ci * c + lax.broadcasted_iota(jnp.int32, (c, 1), 0)
    live = row >= pad
    lf_all = jnp.where(live, lf_ref[...], 0.0)
    k_all = jnp.where(live, k_ref[...], 0.0)
    q_all = q_ref[...]
    v_all = v_ref[...]

    wmat = _decay_sum_matrix(c)
    lf_hi = lf_all.astype(BF16)
    r1 = lf_all - lf_hi.astype(F32)
    lf_mid = r1.astype(BF16)
    lf_lo = (r1 - lf_mid.astype(F32)).astype(BF16)
    sums = (jnp.dot(wmat, lf_hi, preferred_element_type=F32)
            + jnp.dot(wmat, lf_mid, preferred_element_type=F32)
            + jnp.dot(wmat, lf_lo, preferred_element_type=F32))

    levels = _levels(c)
    t_i = lax.broadcasted_iota(jnp.int32, (c, c), 0)
    s_i = lax.broadcasted_iota(jnp.int32, (c, c), 1)
    row_i = lax.broadcasted_iota(jnp.int32, (c, 1), 0)
    sub_i = lax.broadcasted_iota(jnp.int32, (c // BASE, BASE, 1), 1)
    ones_b = jnp.ones((DK, DV), BF16)
    onorm = onorm_ref[...]
    g_all = g_ref[...]

    o_heads = []
    for hd in range(H_A):
        sl = slice(hd * DK, (hd + 1) * DK)
        q = q_all[:, sl]
        k = k_all[:, sl]
        v = v_all[:, sl]
        b = sums[0:c, sl]
        st = st_scr[hd]

        o = lax.dot_general((q * jnp.exp(b)).astype(BF16), st.astype(BF16),
                            (((1,), (1,)), ((), ())), preferred_element_type=F32)

        a_mat = jnp.zeros((c, c), F32)
        for li, m in enumerate(levels):
            eq = sums[(1 + 2 * li) * c:(2 + 2 * li) * c, sl]
            ek = sums[(2 + 2 * li) * c:(3 + 2 * li) * c, sl]
            odd = (_blk(row_i, m) & 1) == 1
            qs = q * jnp.exp(jnp.where(odd, eq, NEG_BIG))
            ks = k * jnp.exp(jnp.where(odd, NEG_BIG, ek))
            am = lax.dot_general(qs.astype(BF16), ks.astype(BF16),
                                 (((1,), (1,)), ((), ())), preferred_element_type=F32)
            pair = ((_blk(t_i, 2 * m) == _blk(s_i, 2 * m))
                    & ((_blk(t_i, m) & 1) == 1) & ((_blk(s_i, m) & 1) == 0))
            a_mat = a_mat + jnp.where(pair, am, 0.0)
        o = o + jnp.dot(a_mat.astype(BF16), v.astype(BF16), preferred_element_type=F32)

        q3 = q.reshape(c // BASE, BASE, DK)
        k3 = k.reshape(c // BASE, BASE, DK)
        v3 = v.reshape(c // BASE, BASE, DV)
        b3 = b.reshape(c // BASE, BASE, DK)
        prods = []
        for d in range(BASE):
            if d == 0:
                prods.append(q3 * k3)
            else:
                kd = pltpu.roll(k3, d, 1)
                bd = pltpu.roll(b3, d, 1)
                dec = jnp.exp(jnp.where(sub_i >= d, b3 - bd, NEG_BIG))
                prods.append(q3 * kd * dec)
        pst = jnp.concatenate(prods, axis=0).reshape(BASE * c, DK)
        asum = jnp.dot(pst.astype(BF16), ones_b, preferred_element_type=F32)
        for d in range(BASE):
            vd = v if d == 0 else pltpu.roll(v3, d, 1).reshape(c, DV)
            o = o + asum[d * c:(d + 1) * c, :] * vd

        bl = b[c - 1:c, :]
        kdec = k * jnp.exp(bl - b)
        upd = lax.dot_general(v.astype(BF16), kdec.astype(BF16),
                              (((0,), (0,)), ((), ())), preferred_element_type=F32)
        st_new = st * jnp.exp(bl) + upd
        st_scr[hd] = st_new

        @pl.when(ci == nc - 1)
        def _():
            sout_ref[0, hd] = st_new.T

        o = o * lax.rsqrt(jnp.mean(o * o, axis=-1, keepdims=True) + EPS)
        o_heads.append(o * onorm[:, sl] * g_all[:, sl])

    zp = zp_ref[...]
    ext = jnp.concatenate([prev_scr[...], zp], axis=0)
    tok = (row - pad).astype(F32)
    avail = jnp.maximum(n_hist + tok + 1.0, 1.0)
    pool_heads = []
    for gi, w in enumerate(POOL_WINDOWS):
        sl = slice(gi * POOL_GW, (gi + 1) * POOL_GW)
        s = ext[:, sl]
        sh = 1
        while sh < w:
            s = s + pltpu.roll(s, sh, 0)
            sh *= 2
        win = s[HIST_ROWS:, :]
        cnt = jnp.minimum(float(w), avail)
        dlt = win / cnt - zp[:, sl]
        po = jnp.dot(dlt.astype(BF16), pw_ref[gi], preferred_element_type=F32)
        pool_heads.append(po * ps_ref[:, sl])
    prev_scr[...] = zp[c - HIST_ROWS:, :]

    mix_ref[...] = jnp.concatenate(o_heads + pool_heads, axis=1).astype(BF16)


def _mixer(acts, s0, hist, out_norm, pool_w_bf, pool_scale, *, batch, t_len, c, pad, n_hist, skip):
    nc = t_len // c
    assert nc * c == t_len and c % BASE == 0 and c >= HIST_ROWS
    blk = lambda b, ci: (b * nc + ci, 0)
    per_b4 = lambda b, ci: (b, 0, 0, 0)
    per_b3 = lambda b, ci: (b, 0, 0)
    c2 = lambda b, ci: (0, 0)
    c3 = lambda b, ci: (0, 0, 0)
    n_main = nc - skip
    mix_map = lambda b, ci: (b * n_main + jnp.maximum(ci - skip, 0), 0)
    kern = functools.partial(_mixer_kernel, c=c, pad=pad, n_hist=float(n_hist))
    return pl.pallas_call(
        kern,
        grid=(batch, nc),
        in_specs=[pl.BlockSpec((c, D_HGRN), blk)] * 6 + [
            pl.BlockSpec((1, H_A, DK, DV), per_b4),
            pl.BlockSpec((1, HIST_ROWS, D_POOL), per_b3),
            pl.BlockSpec((1, D_HGRN), c2),
            pl.BlockSpec((len(POOL_WINDOWS), POOL_GW, POOL_GW), c3),
            pl.BlockSpec((1, D_POOL), c2),
        ],
        out_specs=[
            pl.BlockSpec((c, D_MODEL), mix_map),
            pl.BlockSpec((1, H_A, DK, DV), per_b4),
        ],
        out_shape=[
            jax.ShapeDtypeStruct((batch * n_main * c, D_MODEL), BF16),
            jax.ShapeDtypeStruct((batch, H_A, DK, DV), F32),
        ],
        scratch_shapes=[
            pltpu.VMEM((H_A, DV, DK), F32),
            pltpu.VMEM((HIST_ROWS, D_POOL), F32),
        ],
        compiler_params=_cparams(("parallel", "arbitrary")),
        name="mixer",
    )(*acts, s0, hist, out_norm.reshape(1, D_HGRN), pool_w_bf, pool_scale.reshape(1, D_POOL))


def _outproj_kernel(h_ref, mix_ref, wout_ref, nw_ref, h1_ref, xn_ref):
    h1 = h_ref[...] + jnp.dot(mix_ref[...], wout_ref[...], preferred_element_type=F32)
    h1_ref[...] = h1
    ms = jnp.mean(h1 * h1, axis=-1, keepdims=True)
    xn_ref[...] = h1 * lax.rsqrt(ms + EPS) * nw_ref[...]


def _outproj(h2d, mix, w_out_bf, norm_w):
    n = h2d.shape[0]
    tm = TM_PROJ
    assert n % tm == 0
    row = lambda i: (i, 0)
    const = lambda i: (0, 0)
    out = jax.ShapeDtypeStruct((n, D_MODEL), F32)
    return pl.pallas_call(
        _outproj_kernel,
        grid=(n // tm,),
        in_specs=[
            pl.BlockSpec((tm, D_MODEL), row),
            pl.BlockSpec((tm, D_MODEL), row),
            pl.BlockSpec((D_MODEL, D_MODEL), const),
            pl.BlockSpec((1, D_MODEL), const),
        ],
        out_specs=[pl.BlockSpec((tm, D_MODEL), row)] * 2,
        out_shape=[out, out],
        compiler_params=_cparams(("parallel",)),
        name="outproj",
    )(h2d, mix, w_out_bf, norm_w.reshape(1, D_MODEL))


def _keymat_kernel(sk_ref, wq_ref, m_ref):
    m = lax.dot_general(sk_ref[0], wq_ref[...], (((1,), (1,)), ((), ())),
                        precision=lax.Precision.HIGHEST, preferred_element_type=F32)
    m_ref[...] = m.astype(BF16)


def _keymat(subkeys, w_query):
    nhp = PEER_HEADS * 2
    sk = subkeys.reshape(nhp, N_KEYS, N_KEYS)
    return pl.pallas_call(
        _keymat_kernel,
        grid=(nhp,),
        in_specs=[
            pl.BlockSpec((1, N_KEYS, N_KEYS), lambda i: (i, 0, 0)),
            pl.BlockSpec((D_MODEL, N_KEYS), lambda i: (0, i)),
        ],
        out_specs=pl.BlockSpec((N_KEYS, D_MODEL), lambda i: (i, 0)),
        out_shape=jax.ShapeDtypeStruct((nhp * N_KEYS, D_MODEL), BF16),
        compiler_params=_cparams(("parallel",)),
        name="keymat",
    )(sk, w_query)


_STAIR = [(i, j) for i in range(PEER_TOPK) for j in range(PEER_TOPK) if (i + 1) * (j + 1) <= PEER_TOPK]


def _topk_kernel(xn_ref, m_ref, exp_ref, gate_ref, sc_scr, sv_scr, si_scr):
    tb = xn_ref.shape[0]
    xb = xn_ref[...].astype(BF16)
    sc_scr[...] = lax.dot_general(m_ref[...], xb, (((1,), (1,)), ((), ())), preferred_element_type=F32)
    key_i = lax.broadcasted_iota(jnp.int32, (N_KEYS, tb), 0).astype(F32)

    def head_body(h, carry):
        ss = [sc_scr[pl.ds(pl.multiple_of((2 * h + p) * N_KEYS, N_KEYS), N_KEYS), :] for p in range(2)]
        for i in range(PEER_TOPK):
            for p in range(2):
                s = ss[p]
                mx = jnp.max(s, axis=0, keepdims=True)
                idx = jnp.min(jnp.where(s == mx, key_i, float(N_KEYS)), axis=0, keepdims=True)
                ss[p] = jnp.where(key_i == idx, -jnp.inf, s)
                sv_scr[p, i, pl.ds(h, 1), :] = mx
                si_scr[p, i, pl.ds(h, 1), :] = idx
        return carry

    lax.fori_loop(0, PEER_HEADS, head_body, 0)

    sv0 = [sv_scr[0, i] for i in range(PEER_TOPK)]
    sv1 = [sv_scr[1, i] for i in range(PEER_TOPK)]
    si0 = [si_scr[0, i] for i in range(PEER_TOPK)]
    si1 = [si_scr[1, i] for i in range(PEER_TOPK)]
    cands = [sv0[i] + sv1[j] for (i, j) in _STAIR]
    flats = [float(i * PEER_TOPK + j) for (i, j) in _STAIR]
    tops, experts = [], []
    for r in range(PEER_TOPK):
        mx = functools.reduce(jnp.maximum, cands)
        sel = functools.reduce(jnp.minimum, [jnp.where(cv == mx, fl, 1e9) for cv, fl in zip(cands, flats)])
        cands = [jnp.where(sel == fl, -jnp.inf, cv) for cv, fl in zip(cands, flats)]
        fi = jnp.floor(sel * (1.0 / PEER_TOPK))
        fj = sel - fi * PEER_TOPK
        e0 = functools.reduce(jnp.add, [jnp.where(fi == float(i), si0[i], 0.0) for i in range(PEER_TOPK)])
        e1 = functools.reduce(jnp.add, [jnp.where(fj == float(j), si1[j], 0.0) for j in range(PEER_TOPK)])
        tops.append(mx)
        experts.append((e0 * N_KEYS + e1).astype(jnp.int32))
    ex = [jnp.exp(t - tops[0]) for t in tops]
    den = functools.reduce(jnp.add, ex)
    for r in range(PEER_TOPK):
        exp_ref[0, r] = experts[r]
        gate_ref[0, r] = ex[r] / den


def _topk(xn, keymat):
    n = xn.shape[0]
    tb = TB_TOPK
    assert n % tb == 0
    nb = n // tb
    shp = (nb, PEER_TOPK, PEER_HEADS, tb)
    ospec = pl.BlockSpec((1, PEER_TOPK, PEER_HEADS, tb), lambda i: (i, 0, 0, 0))
    return pl.pallas_call(
        _topk_kernel,
        grid=(nb,),
        in_specs=[
            pl.BlockSpec((tb, D_MODEL), lambda i: (i, 0)),
            pl.BlockSpec(keymat.shape, lambda i: (0, 0)),
        ],
        out_specs=[ospec, ospec],
        out_shape=[jax.ShapeDtypeStruct(shp, jnp.int32), jax.ShapeDtypeStruct(shp, F32)],
        scratch_shapes=[
            pltpu.VMEM((2 * PEER_HEADS * N_KEYS, tb), F32),
            pltpu.VMEM((2, PEER_TOPK, PEER_HEADS, tb), F32),
            pltpu.VMEM((2, PEER_TOPK, PEER_HEADS, tb), F32),
        ],
        compiler_params=_cparams(("parallel",)),
        name="topk",
    )(xn, keymat)


ROW_TILES = 2 * D_MODEL // 128
HALF_TILES = D_MODEL // 128


SUB = 8


def _as_tile(row):
    return jnp.concatenate([row[:, c * 128:(c + 1) * 128] for c in range(HALF_TILES)], axis=0)


def _as_row(tile):
    return jnp.concatenate([tile[c:c + 1, :] for c in range(HALF_TILES)], axis=1)


def _peer_kernel(first_ref, ahead_ref, gate_ref, xn_ref, h1_ref, nf_ref, gsum_ref, tab_ref, y_ref,
                 buf, w_scr, sem):
    tb = xn_ref.shape[0]
    step = pl.program_id(0)

    def gather(row_ref, j, slot):
        for e in range(HITS):
            pltpu.make_async_copy(tab_ref.at[row_ref[j, e]], buf.at[slot, e], sem.at[slot]).start()

    def gather_wait(slot):
        pltpu.make_async_copy(tab_ref.at[pl.ds(0, HITS)], buf.at[slot], sem.at[slot]).wait()

    @pl.when(step == 0)
    def _():
        for t in range(AHEAD):
            gather(first_ref, t, t % N_SLOTS)

    eye = (lax.broadcasted_iota(jnp.int32, (HITS, HITS), 0)
           == lax.broadcasted_iota(jnp.int32, (HITS, HITS), 1)).astype(F32)
    nf = _as_tile(nf_ref[...])

    def group_body(g, carry):
        base = pl.multiple_of(g * SUB, SUB)
        x8 = xn_ref[pl.ds(base, SUB), :]
        h8 = h1_ref[pl.ds(base, SUB), :]
        g8 = gate_ref[pl.ds(base, SUB), :]
        rows = []
        for i in range(SUB):
            slot = i % N_SLOTS
            gather(ahead_ref, base + i, (i + AHEAD) % N_SLOTS)
            gather_wait(slot)
            x3 = _as_tile(x8[i:i + 1, :])
            u3 = buf[slot, :, 0:HALF_TILES, :]
            prod = (u3 * x3[None]).reshape(HITS * HALF_TILES, 128)
            part = jnp.dot(gsum_ref[...], prod.astype(BF16), preferred_element_type=F32)
            a_col = jnp.sum(part, axis=1, keepdims=True)
            g_col = jnp.sum(eye * g8[i:i + 1, :], axis=1, keepdims=True)
            w_col = g_col * (0.5 * a_col * (1.0 + lax.erf(a_col * (2.0 ** -0.5))))
            w_scr[i] = jnp.broadcast_to(w_col, (HITS, 128))
            accs = [jnp.zeros((HALF_TILES, 128), F32) for _ in range(4)]
            for e in range(HITS):
                accs[e % 4] = accs[e % 4] + w_scr[i, pl.ds(e, 1), :] * buf[slot, e, HALF_TILES:ROW_TILES, :]
            y3 = _as_tile(h8[i:i + 1, :]) + ((accs[0] + accs[1]) + (accs[2] + accs[3]))
            ms = jnp.sum(jnp.sum(y3 * y3, axis=1, keepdims=True), axis=0, keepdims=True) * (1.0 / D_MODEL)
            rows.append(_as_row(y3 * lax.rsqrt(ms + EPS) * nf))
        y_ref[pl.ds(base, SUB), :] = jnp.concatenate(rows, axis=0)
        return carry

    lax.fori_loop(0, tb // SUB, group_body, 0)

    @pl.when(step == pl.num_programs(0) - 1)
    def _():
        for t in range(AHEAD):
            gather_wait(t % N_SLOTS)


def _peer(idx, gates, xn, h1, norm_final, gsum, table):
    n = xn.shape[0]
    tb = TB_PEER
    assert n % tb == 0 and tb % SUB == 0 and SUB % N_SLOTS == 0 and AHEAD < N_SLOTS
    ahead = jnp.concatenate([idx[AHEAD:], jnp.broadcast_to(idx[n - 1:], (AHEAD, HITS))], axis=0)
    first = idx[:SUB]
    row = lambda i: (i, 0)
    const = lambda i: (0, 0)
    return pl.pallas_call(
        _peer_kernel,
        grid=(n // tb,),
        in_specs=[
            pl.BlockSpec((SUB, HITS), const, memory_space=pltpu.SMEM),
            pl.BlockSpec((tb, HITS), row, memory_space=pltpu.SMEM),
            pl.BlockSpec((tb, HITS), row),
            pl.BlockSpec((tb, D_MODEL), row),
            pl.BlockSpec((tb, D_MODEL), row),
            pl.BlockSpec((1, D_MODEL), const),
            pl.BlockSpec(gsum.shape, const),
            pl.BlockSpec(memory_space=pl.ANY),
        ],
        out_specs=pl.BlockSpec((tb, D_MODEL), row),
        out_shape=jax.ShapeDtypeStruct((n, D_MODEL), F32),
        scratch_shapes=[
            pltpu.VMEM((N_SLOTS, HITS, ROW_TILES, 128), F32),
            pltpu.VMEM((SUB, HITS, 128), F32),
            pltpu.SemaphoreType.DMA((N_SLOTS,)),
        ],
        compiler_params=_cparams(("arbitrary",)),
        name="peer",
    )(first, ahead, gates, xn, h1, norm_final.reshape(1, D_MODEL), gsum, table)


def _token_major(a):
    nb, r, h, t = a.shape
    return jnp.transpose(a, (0, 3, 2, 1)).reshape(nb * t, h * r)


def _stream(h_pad, h_main, s0, hist, n_hist, pad, skip, c, consts):
    (norm_mix, w_in_bf, lb_logits, out_norm, pool_w_bf, pool_scale, w_out_bf, norm_ffn,
     keymat, gsum, table, norm_final) = consts
    batch, t_len, _ = h_pad.shape
    acts = _inproj(h_pad.reshape(batch * t_len, D_MODEL), norm_mix, w_in_bf, lb_logits)
    mix, s_new = _mixer(acts, s0, hist, out_norm, pool_w_bf, pool_scale,
                        batch=batch, t_len=t_len, c=c, pad=pad, n_hist=n_hist, skip=skip)
    zp = acts[5].reshape(batch, t_len, D_POOL)
    new_hist = zp[:, t_len - POOL_HIST:, :]
    hm = h_main.reshape(-1, D_MODEL)
    h1, xn = _outproj(hm, mix, w_out_bf, norm_ffn)
    experts, gates = _topk(xn, keymat)
    y = _peer(_token_major(experts), _token_major(gates), xn, h1, norm_final, gsum, table)
    return y.reshape(h_main.shape), s_new, new_hist


def kernel(x_prompt, x_sample, state_hgrn, cache_pool, meta_tokens, norm_mix, w_in, hgrn_lb_logits,
           hgrn_out_norm, pool_w, pool_scale, w_out, norm_ffn, peer_w_query, peer_subkeys, peer_u,
           peer_v, norm_final):
    assert state_hgrn.shape[0] == 1, "single trunk layer"
    bp, seq, _ = x_prompt.shape
    bs, dseq, _ = x_sample.shape
    dt = x_prompt.dtype

    keymat = _keymat(peer_subkeys[0], peer_w_query[0])
    table = jnp.concatenate([peer_u[0], peer_v[0]], axis=1).reshape(N_EXPERTS, ROW_TILES, 128)
    r = jnp.arange(HITS * HALF_TILES, dtype=jnp.int32)[None, :] // HALF_TILES
    gsum = (r == jnp.arange(HITS, dtype=jnp.int32)[:, None]).astype(BF16)
    consts = (norm_mix[0], w_in[0].astype(BF16), hgrn_lb_logits, hgrn_out_norm[0],
              pool_w[0].astype(BF16), pool_scale[0], w_out[0].astype(BF16), norm_ffn[0],
              keymat, gsum, table, norm_final)

    pad = CHUNK - N_META
    head = jnp.concatenate([jnp.zeros((bp, pad, D_MODEL), dt),
                            jnp.broadcast_to(meta_tokens[None].astype(dt), (bp, N_META, D_MODEL))], axis=1)
    hp = jnp.concatenate([head, x_prompt], axis=1)
    s_p0 = jnp.zeros((bp, H_A, DK, DV), F32)
    hist_p0 = jnp.zeros((bp, HIST_ROWS, D_POOL), dt)
    y_p, s_p, c_p = _stream(hp, x_prompt, s_p0, hist_p0, 0, pad, 1, CHUNK, consts)

    hist_s0 = jnp.concatenate([jnp.zeros((bs, HIST_ROWS - POOL_HIST, D_POOL), dt), cache_pool[0]], axis=1)
    y_s, s_s, c_s = _stream(x_sample, x_sample, state_hgrn[0], hist_s0, POOL_HIST, 0, 0, dseq, consts)

    return (y_p, y_s, s_p[None].astype(dt), c_p[None], s_s[None].astype(dt), c_s[None])
```

```python
import functools

import jax
import jax.numpy as jnp
from jax import lax
from jax.experimental import pallas as pl
from jax.experimental.pallas import tpu as pltpu

F32 = jnp.float32
BF16 = jnp.bfloat16

D_MODEL = 1024
N_META = 16
D_HGRN = 512
D_POOL = 512
H_A = 4
DK = 128
DV = 128
POOL_WINDOWS = (2, 4, 8, 16)
POOL_GW = 128
POOL_HIST = 15
D_IN_PROJ = 4 * D_HGRN + D_POOL
PEER_HEADS = 8
N_KEYS = 128
N_EXPERTS = N_KEYS * N_KEYS
PEER_TOPK = 16
EPS = 1e-6
CHUNK = 64
HIST_ROWS = 16
BASE = 8

VMEM_LIMIT_BYTES = 48 * 1024 * 1024

TM_PROJ = 256
TB_TOPK = 128
TB_PEER = 64
N_SLOTS = 16
AHEAD = 8
CHUNK_HITS = 16
HITS = PEER_HEADS * PEER_TOPK
NEG_BIG = -1e30


def _cparams(sem):
    return pltpu.CompilerParams(dimension_semantics=sem, vmem_limit_bytes=VMEM_LIMIT_BYTES)


def _inproj_kernel(h_ref, nw_ref, win_ref, lbl_ref, q_ref, k_ref, lf_ref, v_ref, g_ref, zp_ref):
    x = h_ref[...]
    ms = jnp.mean(x * x, axis=-1, keepdims=True)
    xn = x * lax.rsqrt(ms + EPS) * nw_ref[...]
    proj = jnp.dot(xn.astype(BF16), win_ref[...], preferred_element_type=F32)
    zq = proj[:, 0:D_HGRN]
    zf = proj[:, D_HGRN:2 * D_HGRN]
    zi = proj[:, 2 * D_HGRN:3 * D_HGRN]
    zg = proj[:, 3 * D_HGRN:4 * D_HGRN]
    zp = proj[:, 4 * D_HGRN:]
    lg = lbl_ref[...]
    e = jnp.exp(lg - jnp.max(lg, axis=0, keepdims=True))
    lb = e[0:1, :] / jnp.sum(e, axis=0, keepdims=True)
    q_ref[...] = jax.nn.silu(zq) * (DK ** -0.5)
    lf_ref[...] = jnp.log(lb + (1.0 - lb) * jax.nn.sigmoid(zf))
    k_ref[...] = (1.0 - lb) * jax.nn.sigmoid(-zf)
    v_ref[...] = zi
    g_ref[...] = jax.nn.silu(zg)
    zp_ref[...] = zp


def _inproj(h2d, norm_w, w_in_bf, lb_logits):
    n = h2d.shape[0]
    tm = TM_PROJ
    assert n % tm == 0
    row = lambda i: (i, 0)
    const = lambda i: (0, 0)
    out = jax.ShapeDtypeStruct((n, D_HGRN), F32)
    return pl.pallas_call(
        _inproj_kernel,
        grid=(n // tm,),
        in_specs=[
            pl.BlockSpec((tm, D_MODEL), row),
            pl.BlockSpec((1, D_MODEL), const),
            pl.BlockSpec((D_MODEL, D_IN_PROJ), const),
            pl.BlockSpec(lb_logits.shape, const),
        ],
        out_specs=[pl.BlockSpec((tm, D_HGRN), row)] * 6,
        out_shape=[out] * 6,
        compiler_params=_cparams(("parallel",)),
        name="inproj",
    )(h2d, norm_w.reshape(1, D_MODEL), w_in_bf, lb_logits)


def _levels(c):
    out, m = [], BASE
    while m < c:
        out.append(m)
        m *= 2
    return out


def _blk(x, m):
    return jnp.right_shift(x, m.bit_length() - 1)


def _decay_sum_matrix(c):
    t = lax.broadcasted_iota(jnp.int32, (c, c), 0)
    r = lax.broadcasted_iota(jnp.int32, (c, c), 1)
    mats = [r <= t]
    for m in _levels(c):
        lo = _blk(t, m) * m
        mats.append((r >= lo) & (r <= t))
        mats.append((r > t) & (r <= lo + m - 1))
    return jnp.concatenate([x.astype(F32) for x in mats], axis=0).astype(BF16)


def _mixer_kernel(q_ref, k_ref, lf_ref, v_ref, g_ref, zp_ref, s0_ref, hist_ref, onorm_ref,
                  pw_ref, ps_ref, mix_ref, sout_ref, st_scr, prev_scr, *, c, pad, n_hist):
    ci = pl.program_id(1)
    nc = pl.num_programs(1)

    @pl.when(ci == 0)
    def _():
        for hd in range(H_A):
            st_scr[hd] = s0_ref[0, hd].T
        prev_scr[...] = hist_ref[0]

    row = ci * c + lax.broadcasted_iota(jnp.int32, (c, 1), 0)
    live = row >= pad
    lf_all = jnp.where(live, lf_ref[...], 0.0)
    k_all = jnp.where(live, k_ref[...], 0.0)
    q_all = q_ref[...]
    v_all = v_ref[...]

    wmat = _decay_sum_matrix(c)
    lf_hi = lf_all.astype(BF16)
    r1 = lf_all - lf_hi.astype(F32)
    lf_mid = r1.astype(BF16)
    lf_lo = (r1 - lf_mid.astype(F32)).astype(BF16)
    sums = (jnp.dot(wmat, lf_hi, preferred_element_type=F32)
            + jnp.dot(wmat, lf_mid, preferred_element_type=F32)
            + jnp.dot(wmat, lf_lo, preferred_element_type=F32))

    levels = _levels(c)
    t_i = lax.broadcasted_iota(jnp.int32, (c, c), 0)
    s_i = lax.broadcasted_iota(jnp.int32, (c, c), 1)
    row_i = lax.broadcasted_iota(jnp.int32, (c, 1), 0)
    sub_i = lax.broadcasted_iota(jnp.int32, (c // BASE, BASE, 1), 1)
    ones_b = jnp.ones((DK, DV), BF16)
    onorm = onorm_ref[...]
    g_all = g_ref[...]

    o_heads = []
    for hd in range(H_A):
        sl = slice(hd * DK, (hd + 1) * DK)
        q = q_all[:, sl]
        k = k_all[:, sl]
        v = v_all[:, sl]
        b = sums[0:c, sl]
        st = st_scr[hd]

        o = lax.dot_general((q * jnp.exp(b)).astype(BF16), st.astype(BF16),
                            (((1,), (1,)), ((), ())), preferred_element_type=F32)

        a_mat = jnp.zeros((c, c), F32)
        for li, m in enumerate(levels):
            eq = sums[(1 + 2 * li) * c:(2 + 2 * li) * c, sl]
            ek = sums[(2 + 2 * li) * c:(3 + 2 * li) * c, sl]
            odd = (_blk(row_i, m) & 1) == 1
            qs = q * jnp.exp(jnp.where(odd, eq, NEG_BIG))
            ks = k * jnp.exp(jnp.where(odd, NEG_BIG, ek))
            am = lax.dot_general(qs.astype(BF16), ks.astype(BF16),
                                 (((1,), (1,)), ((), ())), preferred_element_type=F32)
            pair = ((_blk(t_i, 2 * m) == _blk(s_i, 2 * m))
                    & ((_blk(t_i, m) & 1) == 1) & ((_blk(s_i, m) & 1) == 0))
            a_mat = a_mat + jnp.where(pair, am, 0.0)
        o = o + jnp.dot(a_mat.astype(BF16), v.astype(BF16), preferred_element_type=F32)

        q3 = q.reshape(c // BASE, BASE, DK)
        k3 = k.reshape(c // BASE, BASE, DK)
        v3 = v.reshape(c // BASE, BASE, DV)
        b3 = b.reshape(c // BASE, BASE, DK)
        prods = []
        for d in range(BASE):
            if d == 0:
                prods.append(q3 * k3)
            else:
                kd = pltpu.roll(k3, d, 1)
                bd = pltpu.roll(b3, d, 1)
                dec = jnp.exp(jnp.where(sub_i >= d, b3 - bd, NEG_BIG))
                prods.append(q3 * kd * dec)
        pst = jnp.concatenate(prods, axis=0).reshape(BASE * c, DK)
        asum = jnp.dot(pst.astype(BF16), ones_b, preferred_element_type=F32)
        for d in range(BASE):
            vd = v if d == 0 else pltpu.roll(v3, d, 1).reshape(c, DV)
            o = o + asum[d * c:(d + 1) * c, :] * vd

        bl = b[c - 1:c, :]
        kdec = k * jnp.exp(bl - b)
        upd = lax.dot_general(v.astype(BF16), kdec.astype(BF16),
                              (((0,), (0,)), ((), ())), preferred_element_type=F32)
        st_new = st * jnp.exp(bl) + upd
        st_scr[hd] = st_new

        @pl.when(ci == nc - 1)
        def _():
            sout_ref[0, hd] = st_new.T

        o = o * lax.rsqrt(jnp.mean(o * o, axis=-1, keepdims=True) + EPS)
        o_heads.append(o * onorm[:, sl] * g_all[:, sl])

    zp = zp_ref[...]
    ext = jnp.concatenate([prev_scr[...], zp], axis=0)
    tok = (row - pad).astype(F32)
    avail = jnp.maximum(n_hist + tok + 1.0, 1.0)
    pool_heads = []
    for gi, w in enumerate(POOL_WINDOWS):
        sl = slice(gi * POOL_GW, (gi + 1) * POOL_GW)
        s = ext[:, sl]
        sh = 1
        while sh < w:
            s = s + pltpu.roll(s, sh, 0)
            sh *= 2
        win = s[HIST_ROWS:, :]
        cnt = jnp.minimum(float(w), avail)
        dlt = win / cnt - zp[:, sl]
        po = jnp.dot(dlt.astype(BF16), pw_ref[gi], preferred_element_type=F32)
        pool_heads.append(po * ps_ref[:, sl])
    prev_scr[...] = zp[c - HIST_ROWS:, :]

    mix_ref[...] = jnp.concatenate(o_heads + pool_heads, axis=1).astype(BF16)


def _mixer(acts, s0, hist, out_norm, pool_w_bf, pool_scale, *, batch, t_len, c, pad, n_hist, skip):
    nc = t_len // c
    assert nc * c == t_len and c % BASE == 0 and c >= HIST_ROWS
    blk = lambda b, ci: (b * nc + ci, 0)
    per_b4 = lambda b, ci: (b, 0, 0, 0)
    per_b3 = lambda b, ci: (b, 0, 0)
    c2 = lambda b, ci: (0, 0)
    c3 = lambda b, ci: (0, 0, 0)
    n_main = nc - skip
    mix_map = lambda b, ci: (b * n_main + jnp.maximum(ci - skip, 0), 0)
    kern = functools.partial(_mixer_kernel, c=c, pad=pad, n_hist=float(n_hist))
    return pl.pallas_call(
        kern,
        grid=(batch, nc),
        in_specs=[pl.BlockSpec((c, D_HGRN), blk)] * 6 + [
            pl.BlockSpec((1, H_A, DK, DV), per_b4),
            pl.BlockSpec((1, HIST_ROWS, D_POOL), per_b3),
            pl.BlockSpec((1, D_HGRN), c2),
            pl.BlockSpec((len(POOL_WINDOWS), POOL_GW, POOL_GW), c3),
            pl.BlockSpec((1, D_POOL), c2),
        ],
        out_specs=[
            pl.BlockSpec((c, D_MODEL), mix_map),
            pl.BlockSpec((1, H_A, DK, DV), per_b4),
        ],
        out_shape=[
            jax.ShapeDtypeStruct((batch * n_main * c, D_MODEL), BF16),
            jax.ShapeDtypeStruct((batch, H_A, DK, DV), F32),
        ],
        scratch_shapes=[
            pltpu.VMEM((H_A, DV, DK), F32),
            pltpu.VMEM((HIST_ROWS, D_POOL), F32),
        ],
        compiler_params=_cparams(("parallel", "arbitrary")),
        name="mixer",
    )(*acts, s0, hist, out_norm.reshape(1, D_HGRN), pool_w_bf, pool_scale.reshape(1, D_POOL))


def _outproj_kernel(h_ref, mix_ref, wout_ref, nw_ref, h1_ref, xn_ref):
    h1 = h_ref[...] + jnp.dot(mix_ref[...], wout_ref[...], preferred_element_type=F32)
    h1_ref[...] = h1
    ms = jnp.mean(h1 * h1, axis=-1, keepdims=True)
    xn_ref[...] = h1 * lax.rsqrt(ms + EPS) * nw_ref[...]


def _outproj(h2d, mix, w_out_bf, norm_w):
    n = h2d.shape[0]
    tm = TM_PROJ
    assert n % tm == 0
    row = lambda i: (i, 0)
    const = lambda i: (0, 0)
    out = jax.ShapeDtypeStruct((n, D_MODEL), F32)
    return pl.pallas_call(
        _outproj_kernel,
        grid=(n // tm,),
        in_specs=[
            pl.BlockSpec((tm, D_MODEL), row),
            pl.BlockSpec((tm, D_MODEL), row),
            pl.BlockSpec((D_MODEL, D_MODEL), const),
            pl.BlockSpec((1, D_MODEL), const),
        ],
        out_specs=[pl.BlockSpec((tm, D_MODEL), row)] * 2,
        out_shape=[out, out],
        compiler_params=_cparams(("parallel",)),
        name="outproj",
    )(h2d, mix, w_out_bf, norm_w.reshape(1, D_MODEL))


def _keymat_kernel(sk_ref, wq_ref, m_ref):
    m = lax.dot_general(sk_ref[0], wq_ref[...], (((1,), (1,)), ((), ())),
                        precision=lax.Precision.HIGHEST, preferred_element_type=F32)
    m_ref[...] = m.astype(BF16)


def _keymat(subkeys, w_query):
    nhp = PEER_HEADS * 2
    sk = subkeys.reshape(nhp, N_KEYS, N_KEYS)
    return pl.pallas_call(
        _keymat_kernel,
        grid=(nhp,),
        in_specs=[
            pl.BlockSpec((1, N_KEYS, N_KEYS), lambda i: (i, 0, 0)),
            pl.BlockSpec((D_MODEL, N_KEYS), lambda i: (0, i)),
        ],
        out_specs=pl.BlockSpec((N_KEYS, D_MODEL), lambda i: (i, 0)),
        out_shape=jax.ShapeDtypeStruct((nhp * N_KEYS, D_MODEL), BF16),
        compiler_params=_cparams(("parallel",)),
        name="keymat",
    )(sk, w_query)


_STAIR = [(i, j) for i in range(PEER_TOPK) for j in range(PEER_TOPK) if (i + 1) * (j + 1) <= PEER_TOPK]


def _topk_kernel(xn_ref, m_ref, exp_ref, gate_ref, sc_scr, sv_scr, si_scr):
    tb = xn_ref.shape[0]
    xb = xn_ref[...].astype(BF16)
    sc_scr[...] = lax.dot_general(m_ref[...], xb, (((1,), (1,)), ((), ())), preferred_element_type=F32)
    key_i = lax.broadcasted_iota(jnp.int32, (N_KEYS, tb), 0).astype(F32)

    def head_body(h, carry):
        ss = [sc_scr[pl.ds(pl.multiple_of((2 * h + p) * N_KEYS, N_KEYS), N_KEYS), :] for p in range(2)]
        for i in range(PEER_TOPK):
            for p in range(2):
                s = ss[p]
                mx = jnp.max(s, axis=0, keepdims=True)
                idx = jnp.min(jnp.where(s == mx, key_i, float(N_KEYS)), axis=0, keepdims=True)
                ss[p] = jnp.where(key_i == idx, -jnp.inf, s)
                sv_scr[p, i, pl.ds(h, 1), :] = mx
                si_scr[p, i, pl.ds(h, 1), :] = idx
        return carry

    lax.fori_loop(0, PEER_HEADS, head_body, 0)

    sv0 = [sv_scr[0, i] for i in range(PEER_TOPK)]
    sv1 = [sv_scr[1, i] for i in range(PEER_TOPK)]
    si0 = [si_scr[0, i] for i in range(PEER_TOPK)]
    si1 = [si_scr[1, i] for i in range(PEER_TOPK)]
    cands = [sv0[i] + sv1[j] for (i, j) in _STAIR]
    flats = [float(i * PEER_TOPK + j) for (i, j) in _STAIR]
    tops, experts = [], []
    for r in range(PEER_TOPK):
        mx = functools.reduce(jnp.maximum, cands)
        sel = functools.reduce(jnp.minimum, [jnp.where(cv == mx, fl, 1e9) for cv, fl in zip(cands, flats)])
        cands = [jnp.where(sel == fl, -jnp.inf, cv) for cv, fl in zip(cands, flats)]
        fi = jnp.floor(sel * (1.0 / PEER_TOPK))
        fj = sel - fi * PEER_TOPK
        e0 = functools.reduce(jnp.add, [jnp.where(fi == float(i), si0[i], 0.0) for i in range(PEER_TOPK)])
        e1 = functools.reduce(jnp.add, [jnp.where(fj == float(j), si1[j], 0.0) for j in range(PEER_TOPK)])
        tops.append(mx)
        experts.append((e0 * N_KEYS + e1).astype(jnp.int32))
    ex = [jnp.exp(t - tops[0]) for t in tops]
    den = functools.reduce(jnp.add, ex)
    for r in range(PEER_TOPK):
        exp_ref[0, r] = experts[r]
        gate_ref[0, r] = ex[r] / den


def _topk(xn, keymat):
    n = xn.shape[0]
    tb = TB_TOPK
    assert n % tb == 0
    nb = n // tb
    shp = (nb, PEER_TOPK, PEER_HEADS, tb)
    ospec = pl.BlockSpec((1, PEER_TOPK, PEER_HEADS, tb), lambda i: (i, 0, 0, 0))
    return pl.pallas_call(
        _topk_kernel,
        grid=(nb,),
        in_specs=[
            pl.BlockSpec((tb, D_MODEL), lambda i: (i, 0)),
            pl.BlockSpec(keymat.shape, lambda i: (0, 0)),
        ],
        out_specs=[ospec, ospec],
        out_shape=[jax.ShapeDtypeStruct(shp, jnp.int32), jax.ShapeDtypeStruct(shp, F32)],
        scratch_shapes=[
            pltpu.VMEM((2 * PEER_HEADS * N_KEYS, tb), F32),
            pltpu.VMEM((2, PEER_TOPK, PEER_HEADS, tb), F32),
            pltpu.VMEM((2, PEER_TOPK, PEER_HEADS, tb), F32),
        ],
        compiler_params=_cparams(("parallel",)),
        name="topk",
    )(xn, keymat)


ROW_TILES = 2 * D_MODEL // 128
HALF_TILES = D_MODEL // 128


SUB = 8


def _as_tile(row):
    return jnp.concatenate([row[:, c * 128:(c + 1) * 128] for c in range(HALF_TILES)], axis=0)


def _as_row(tile):
    return jnp.concatenate([tile[c:c + 1, :] for c in range(HALF_TILES)], axis=1)


def _peer_kernel(first_ref, ahead_ref, gate_ref, xn_ref, h1_ref, nf_ref, gsum_ref, tab_ref, y_ref,
                 buf, w_scr, sem):
    tb = xn_ref.shape[0]
    step = pl.program_id(0)

    def start_row(row_ref, j, slot, e):
        pltpu.make_async_copy(tab_ref.at[row_ref[j, e]], buf.at[slot, e], sem.at[slot]).start(priority=e % 2)

    def gather_wait(slot):
        pltpu.make_async_copy(tab_ref.at[pl.ds(0, HITS)], buf.at[slot], sem.at[slot]).wait()

    @pl.when(step == 0)
    def _():
        for t in range(AHEAD):
            for e in range(HITS):
                start_row(first_ref, t, t, e)

    eye = (lax.broadcasted_iota(jnp.int32, (HITS, HITS), 0)
           == lax.broadcasted_iota(jnp.int32, (HITS, HITS), 1)).astype(F32)
    nf = _as_tile(nf_ref[...])
    gs = gsum_ref[...]

    def group_body(g, carry):
        base = pl.multiple_of(g * SUB, SUB)
        cur = (g % 2) * SUB
        nxt = SUB - cur
        x8 = xn_ref[pl.ds(base, SUB), :]
        h8 = h1_ref[pl.ds(base, SUB), :]
        g8 = gate_ref[pl.ds(base, SUB), :]

        for i in range(SUB):
            gather_wait(cur + i)
        parts = []
        for i in range(SUB):
            x3 = _as_tile(x8[i:i + 1, :])
            chunks = []
            for k in range(HITS // CHUNK_HITS):
                prods = []
                for e in range(k * CHUNK_HITS, (k + 1) * CHUNK_HITS):
                    start_row(ahead_ref, base + i, nxt + i, e)
                    prods.append(buf[cur + i, e, 0:HALF_TILES, :] * x3)
                pk = jnp.concatenate(prods, axis=0).astype(BF16)
                chunks.append(jnp.dot(gs, pk, preferred_element_type=F32))
            parts.append(jnp.concatenate(chunks, axis=0))

        for i in range(SUB):
            a_col = jnp.sum(parts[i], axis=1, keepdims=True)
            g_col = jnp.sum(eye * g8[i:i + 1, :], axis=1, keepdims=True)
            w_col = g_col * (0.5 * a_col * (1.0 + lax.erf(a_col * (2.0 ** -0.5))))
            w_scr[i] = jnp.broadcast_to(w_col, (HITS, 128))

        rows = []
        for i in range(SUB):
            accs = [jnp.zeros((HALF_TILES, 128), F32) for _ in range(4)]
            for e in range(HITS):
                accs[e % 4] = accs[e % 4] + w_scr[i, pl.ds(e, 1), :] * buf[cur + i, e, HALF_TILES:ROW_TILES, :]
            y3 = _as_tile(h8[i:i + 1, :]) + ((accs[0] + accs[1]) + (accs[2] + accs[3]))
            ms = jnp.sum(jnp.sum(y3 * y3, axis=1, keepdims=True), axis=0, keepdims=True) * (1.0 / D_MODEL)
            rows.append(_as_row(y3 * lax.rsqrt(ms + EPS) * nf))
        y_ref[pl.ds(base, SUB), :] = jnp.concatenate(rows, axis=0)
        return carry

    lax.fori_loop(0, tb // SUB, group_body, 0)

    @pl.when(step == pl.num_programs(0) - 1)
    def _():
        for t in range(AHEAD):
            gather_wait(t)


def _peer(idx, gates, xn, h1, norm_final, table):
    n = xn.shape[0]
    tb = TB_PEER
    assert n % (2 * SUB) == 0 and n % tb == 0 and tb % (2 * SUB) == 0
    assert AHEAD == SUB and N_SLOTS == 2 * SUB and HITS % CHUNK_HITS == 0
    r = jnp.arange(CHUNK_HITS * HALF_TILES, dtype=jnp.int32)[None, :] // HALF_TILES
    gsum = (r == jnp.arange(CHUNK_HITS, dtype=jnp.int32)[:, None]).astype(BF16)
    ahead = jnp.concatenate([idx[AHEAD:], jnp.broadcast_to(idx[n - 1:], (AHEAD, HITS))], axis=0)
    first = idx[:SUB]
    row = lambda i: (i, 0)
    const = lambda i: (0, 0)
    return pl.pallas_call(
        _peer_kernel,
        grid=(n // tb,),
        in_specs=[
            pl.BlockSpec((SUB, HITS), const, memory_space=pltpu.SMEM),
            pl.BlockSpec((tb, HITS), row, memory_space=pltpu.SMEM),
            pl.BlockSpec((tb, HITS), row),
            pl.BlockSpec((tb, D_MODEL), row),
            pl.BlockSpec((tb, D_MODEL), row),
            pl.BlockSpec((1, D_MODEL), const),
            pl.BlockSpec(gsum.shape, const),
            pl.BlockSpec(memory_space=pl.ANY),
        ],
        out_specs=pl.BlockSpec((tb, D_MODEL), row),
        out_shape=jax.ShapeDtypeStruct((n, D_MODEL), F32),
        scratch_shapes=[
            pltpu.VMEM((N_SLOTS, HITS, ROW_TILES, 128), F32),
            pltpu.VMEM((SUB, HITS, 128), F32),
            pltpu.SemaphoreType.DMA((N_SLOTS,)),
        ],
        compiler_params=_cparams(("arbitrary",)),
        name="peer",
    )(first, ahead, gates, xn, h1, norm_final.reshape(1, D_MODEL), gsum, table)


def _token_major(a):
    nb, r, h, t = a.shape
    return jnp.transpose(a, (0, 3, 2, 1)).reshape(nb * t, h * r)


def _stream(h_pad, h_main, s0, hist, n_hist, pad, skip, c, consts):
    (norm_mix, w_in_bf, lb_logits, out_norm, pool_w_bf, pool_scale, w_out_bf, norm_ffn,
     keymat, table, norm_final) = consts
    batch, t_len, _ = h_pad.shape
    acts = _inproj(h_pad.reshape(batch * t_len, D_MODEL), norm_mix, w_in_bf, lb_logits)
    mix, s_new = _mixer(acts, s0, hist, out_norm, pool_w_bf, pool_scale,
                        batch=batch, t_len=t_len, c=c, pad=pad, n_hist=n_hist, skip=skip)
    zp = acts[5].reshape(batch, t_len, D_POOL)
    new_hist = zp[:, t_len - POOL_HIST:, :]
    hm = h_main.reshape(-1, D_MODEL)
    h1, xn = _outproj(hm, mix, w_out_bf, norm_ffn)
    experts, gates = _topk(xn, keymat)
    y = _peer(_token_major(experts), _token_major(gates), xn, h1, norm_final, table)
    return y.reshape(h_main.shape), s_new, new_hist


def kernel(x_prompt, x_sample, state_hgrn, cache_pool, meta_tokens, norm_mix, w_in, hgrn_lb_logits,
           hgrn_out_norm, pool_w, pool_scale, w_out, norm_ffn, peer_w_query, peer_subkeys, peer_u,
           peer_v, norm_final):
    assert state_hgrn.shape[0] == 1, "single trunk layer"
    bp, seq, _ = x_prompt.shape
    bs, dseq, _ = x_sample.shape
    dt = x_prompt.dtype

    keymat = _keymat(peer_subkeys[0], peer_w_query[0])
    table = jnp.concatenate([peer_u[0], peer_v[0]], axis=1).reshape(N_EXPERTS, ROW_TILES, 128)
    consts = (norm_mix[0], w_in[0].astype(BF16), hgrn_lb_logits, hgrn_out_norm[0],
              pool_w[0].astype(BF16), pool_scale[0], w_out[0].astype(BF16), norm_ffn[0],
              keymat, table, norm_final)

    pad = CHUNK - N_META
    head = jnp.concatenate([jnp.zeros((bp, pad, D_MODEL), dt),
                            jnp.broadcast_to(meta_tokens[None].astype(dt), (bp, N_META, D_MODEL))], axis=1)
    hp = jnp.concatenate([head, x_prompt], axis=1)
    s_p0 = jnp.zeros((bp, H_A, DK, DV), F32)
    hist_p0 = jnp.zeros((bp, HIST_ROWS, D_POOL), dt)
    y_p, s_p, c_p = _stream(hp, x_prompt, s_p0, hist_p0, 0, pad, 1, CHUNK, consts)

    hist_s0 = jnp.concatenate([jnp.zeros((bs, HIST_ROWS - POOL_HIST, D_POOL), dt), cache_pool[0]], axis=1)
    y_s, s_s, c_s = _stream(x_sample, x_sample, state_hgrn[0], hist_s0, POOL_HIST, 0, 0, dseq, consts)

    return (y_p, y_s, s_p[None].astype(dt), c_p[None], s_s[None].astype(dt), c_s[None])
```

```python
import functools

import jax
import jax.numpy as jnp
from jax import lax
from jax.experimental import pallas as pl
from jax.experimental.pallas import tpu as pltpu

F32 = jnp.float32
BF16 = jnp.bfloat16

D_MODEL = 1024
N_META = 16
D_HGRN = 512
D_POOL = 512
H_A = 4
DK = 128
DV = 128
POOL_WINDOWS = (2, 4, 8, 16)
POOL_GW = 128
POOL_HIST = 15
D_IN_PROJ = 4 * D_HGRN + D_POOL
PEER_HEADS = 8
N_KEYS = 128
N_EXPERTS = N_KEYS * N_KEYS
PEER_TOPK = 16
EPS = 1e-6
CHUNK = 64
HIST_ROWS = 16
BASE = 8

VMEM_LIMIT_BYTES = 48 * 1024 * 1024

TM_PROJ = 256
TB_TOPK = 128
TB_PEER = 64
N_SLOTS = 16
AHEAD = 8
CHUNK_HITS = 16
HITS = PEER_HEADS * PEER_TOPK
NEG_BIG = -1e30


def _cparams(sem):
    return pltpu.CompilerParams(dimension_semantics=sem, vmem_limit_bytes=VMEM_LIMIT_BYTES)


def _inproj_kernel(h_ref, nw_ref, win_ref, lbl_ref, q_ref, k_ref, lf_ref, v_ref, g_ref, zp_ref):
    x = h_ref[...]
    ms = jnp.mean(x * x, axis=-1, keepdims=True)
    xn = x * lax.rsqrt(ms + EPS) * nw_ref[...]
    proj = jnp.dot(xn.astype(BF16), win_ref[...], preferred_element_type=F32)
    zq = proj[:, 0:D_HGRN]
    zf = proj[:, D_HGRN:2 * D_HGRN]
    zi = proj[:, 2 * D_HGRN:3 * D_HGRN]
    zg = proj[:, 3 * D_HGRN:4 * D_HGRN]
    zp = proj[:, 4 * D_HGRN:]
    lg = lbl_ref[...]
    e = jnp.exp(lg - jnp.max(lg, axis=0, keepdims=True))
    lb = e[0:1, :] / jnp.sum(e, axis=0, keepdims=True)
    q_ref[...] = jax.nn.silu(zq) * (DK ** -0.5)
    lf_ref[...] = jnp.log(lb + (1.0 - lb) * jax.nn.sigmoid(zf))
    k_ref[...] = (1.0 - lb) * jax.nn.sigmoid(-zf)
    v_ref[...] = zi
    g_ref[...] = jax.nn.silu(zg)
    zp_ref[...] = zp


def _inproj(h2d, norm_w, w_in_bf, lb_logits):
    n = h2d.shape[0]
    tm = TM_PROJ
    assert n % tm == 0
    row = lambda i: (i, 0)
    const = lambda i: (0, 0)
    out = jax.ShapeDtypeStruct((n, D_HGRN), F32)
    return pl.pallas_call(
        _inproj_kernel,
        grid=(n // tm,),
        in_specs=[
            pl.BlockSpec((tm, D_MODEL), row),
            pl.BlockSpec((1, D_MODEL), const),
            pl.BlockSpec((D_MODEL, D_IN_PROJ), const),
            pl.BlockSpec(lb_logits.shape, const),
        ],
        out_specs=[pl.BlockSpec((tm, D_HGRN), row)] * 6,
        out_shape=[out] * 6,
        compiler_params=_cparams(("parallel",)),
        name="inproj",
    )(h2d, norm_w.reshape(1, D_MODEL), w_in_bf, lb_logits)


def _levels(c):
    out, m = [], BASE
    while m < c:
        out.append(m)
        m *= 2
    return out


def _blk(x, m):
    return jnp.right_shift(x, m.bit_length() - 1)


def _decay_sum_matrix(c):
    t = lax.broadcasted_iota(jnp.int32, (c, c), 0)
    r = lax.broadcasted_iota(jnp.int32, (c, c), 1)
    mats = [r <= t]
    for m in _levels(c):
        lo = _blk(t, m) * m
        mats.append((r >= lo) & (r <= t))
        mats.append((r > t) & (r <= lo + m - 1))
    return jnp.concatenate([x.astype(F32) for x in mats], axis=0).astype(BF16)


def _mixer_kernel(q_ref, k_ref, lf_ref, v_ref, g_ref, zp_ref, s0_ref, hist_ref, onorm_ref,
                  pw_ref, ps_ref, mix_ref, sout_ref, st_scr, prev_scr, *, c, pad, n_hist):
    ci = pl.program_id(1)
    nc = pl.num_programs(1)

    @pl.when(ci == 0)
    def _():
        for hd in range(H_A):
            st_scr[hd] = s0_ref[0, hd].T
        prev_scr[...] = hist_ref[0]

    row = ci * c + lax.broadcasted_iota(jnp.int32, (c, 1), 0)
    live = row >= pad
    lf_all = jnp.where(live, lf_ref[...], 0.0)
    k_all = jnp.where(live, k_ref[...], 0.0)
    q_all = q_ref[...]
    v_all = v_ref[...]

    wmat = _decay_sum_matrix(c)
    lf_hi = lf_all.astype(BF16)
    r1 = lf_all - lf_hi.astype(F32)
    lf_mid = r1.astype(BF16)
    lf_lo = (r1 - lf_mid.astype(F32)).astype(BF16)
    sums = (jnp.dot(wmat, lf_hi, preferred_element_type=F32)
            + jnp.dot(wmat, lf_mid, preferred_element_type=F32)
            + jnp.dot(wmat, lf_lo, preferred_element_type=F32))

    levels = _levels(c)
    t_i = lax.broadcasted_iota(jnp.int32, (c, c), 0)
    s_i = lax.broadcasted_iota(jnp.int32, (c, c), 1)
    row_i = lax.broadcasted_iota(jnp.int32, (c, 1), 0)
    sub_i = lax.broadcasted_iota(jnp.int32, (c // BASE, BASE, 1), 1)
    ones_b = jnp.ones((DK, DV), BF16)
    onorm = onorm_ref[...]
    g_all = g_ref[...]

    o_heads = []
    for hd in range(H_A):
        sl = slice(hd * DK, (hd + 1) * DK)
        q = q_all[:, sl]
        k = k_all[:, sl]
        v = v_all[:, sl]
        b = sums[0:c, sl]
        st = st_scr[hd]

        o = lax.dot_general((q * jnp.exp(b)).astype(BF16), st.astype(BF16),
                            (((1,), (1,)), ((), ())), preferred_element_type=F32)

        a_mat = jnp.zeros((c, c), F32)
        for li, m in enumerate(levels):
            eq = sums[(1 + 2 * li) * c:(2 + 2 * li) * c, sl]
            ek = sums[(2 + 2 * li) * c:(3 + 2 * li) * c, sl]
            odd = (_blk(row_i, m) & 1) == 1
            qs = q * jnp.exp(jnp.where(odd, eq, NEG_BIG))
            ks = k * jnp.exp(jnp.where(odd, NEG_BIG, ek))
            am = lax.dot_general(qs.astype(BF16), ks.astype(BF16),
                                 (((1,), (1,)), ((), ())), preferred_element_type=F32)
            pair = ((_blk(t_i, 2 * m) == _blk(s_i, 2 * m))
                    & ((_blk(t_i, m) & 1) == 1) & ((_blk(s_i, m) & 1) == 0))
            a_mat = a_mat + jnp.where(pair, am, 0.0)
        o = o + jnp.dot(a_mat.astype(BF16), v.astype(BF16), preferred_element_type=F32)

        q3 = q.reshape(c // BASE, BASE, DK)
        k3 = k.reshape(c // BASE, BASE, DK)
        v3 = v.reshape(c // BASE, BASE, DV)
        b3 = b.reshape(c // BASE, BASE, DK)
        prods = []
        for d in range(BASE):
            if d == 0:
                prods.append(q3 * k3)
            else:
                kd = pltpu.roll(k3, d, 1)
                bd = pltpu.roll(b3, d, 1)
                dec = jnp.exp(jnp.where(sub_i >= d, b3 - bd, NEG_BIG))
                prods.append(q3 * kd * dec)
        pst = jnp.concatenate(prods, axis=0).reshape(BASE * c, DK)
        asum = jnp.dot(pst.astype(BF16), ones_b, preferred_element_type=F32)
        for d in range(BASE):
            vd = v if d == 0 else pltpu.roll(v3, d, 1).reshape(c, DV)
            o = o + asum[d * c:(d + 1) * c, :] * vd

        bl = b[c - 1:c, :]
        kdec = k * jnp.exp(bl - b)
        upd = lax.dot_general(v.astype(BF16), kdec.astype(BF16),
                              (((0,), (0,)), ((), ())), preferred_element_type=F32)
        st_new = st * jnp.exp(bl) + upd
        st_scr[hd] = st_new

        @pl.when(ci == nc - 1)
        def _():
            sout_ref[0, hd] = st_new.T

        o = o * lax.rsqrt(jnp.mean(o * o, axis=-1, keepdims=True) + EPS)
        o_heads.append(o * onorm[:, sl] * g_all[:, sl])

    zp = zp_ref[...]
    ext = jnp.concatenate([prev_scr[...], zp], axis=0)
    tok = (row - pad).astype(F32)
    avail = jnp.maximum(n_hist + tok + 1.0, 1.0)
    pool_heads = []
    for gi, w in enumerate(POOL_WINDOWS):
        sl = slice(gi * POOL_GW, (gi + 1) * POOL_GW)
        s = ext[:, sl]
        sh = 1
        while sh < w:
            s = s + pltpu.roll(s, sh, 0)
            sh *= 2
        win = s[HIST_ROWS:, :]
        cnt = jnp.minimum(float(w), avail)
        dlt = win / cnt - zp[:, sl]
        po = jnp.dot(dlt.astype(BF16), pw_ref[gi], preferred_element_type=F32)
        pool_heads.append(po * ps_ref[:, sl])
    prev_scr[...] = zp[c - HIST_ROWS:, :]

    mix_ref[...] = jnp.concatenate(o_heads + pool_heads, axis=1).astype(BF16)


def _mixer(acts, s0, hist, out_norm, pool_w_bf, pool_scale, *, batch, t_len, c, pad, n_hist, skip):
    nc = t_len // c
    assert nc * c == t_len and c % BASE == 0 and c >= HIST_ROWS
    blk = lambda b, ci: (b * nc + ci, 0)
    per_b4 = lambda b, ci: (b, 0, 0, 0)
    per_b3 = lambda b, ci: (b, 0, 0)
    c2 = lambda b, ci: (0, 0)
    c3 = lambda b, ci: (0, 0, 0)
    n_main = nc - skip
    mix_map = lambda b, ci: (b * n_main + jnp.maximum(ci - skip, 0), 0)
    kern = functools.partial(_mixer_kernel, c=c, pad=pad, n_hist=float(n_hist))
    return pl.pallas_call(
        kern,
        grid=(batch, nc),
        in_specs=[pl.BlockSpec((c, D_HGRN), blk)] * 6 + [
            pl.BlockSpec((1, H_A, DK, DV), per_b4),
            pl.BlockSpec((1, HIST_ROWS, D_POOL), per_b3),
            pl.BlockSpec((1, D_HGRN), c2),
            pl.BlockSpec((len(POOL_WINDOWS), POOL_GW, POOL_GW), c3),
            pl.BlockSpec((1, D_POOL), c2),
        ],
        out_specs=[
            pl.BlockSpec((c, D_MODEL), mix_map),
            pl.BlockSpec((1, H_A, DK, DV), per_b4),
        ],
        out_shape=[
            jax.ShapeDtypeStruct((batch * n_main * c, D_MODEL), BF16),
            jax.ShapeDtypeStruct((batch, H_A, DK, DV), F32),
        ],
        scratch_shapes=[
            pltpu.VMEM((H_A, DV, DK), F32),
            pltpu.VMEM((HIST_ROWS, D_POOL), F32),
        ],
        compiler_params=_cparams(("parallel", "arbitrary")),
        name="mixer",
    )(*acts, s0, hist, out_norm.reshape(1, D_HGRN), pool_w_bf, pool_scale.reshape(1, D_POOL))


def _outproj_kernel(h_ref, mix_ref, wout_ref, nw_ref, h1_ref, xn_ref):
    h1 = h_ref[...] + jnp.dot(mix_ref[...], wout_ref[...], preferred_element_type=F32)
    h1_ref[...] = h1
    ms = jnp.mean(h1 * h1, axis=-1, keepdims=True)
    xn_ref[...] = h1 * lax.rsqrt(ms + EPS) * nw_ref[...]


def _outproj(h2d, mix, w_out_bf, norm_w):
    n = h2d.shape[0]
    tm = TM_PROJ
    assert n % tm == 0
    row = lambda i: (i, 0)
    const = lambda i: (0, 0)
    out = jax.ShapeDtypeStruct((n, D_MODEL), F32)
    return pl.pallas_call(
        _outproj_kernel,
        grid=(n // tm,),
        in_specs=[
            pl.BlockSpec((tm, D_MODEL), row),
            pl.BlockSpec((tm, D_MODEL), row),
            pl.BlockSpec((D_MODEL, D_MODEL), const),
            pl.BlockSpec((1, D_MODEL), const),
        ],
        out_specs=[pl.BlockSpec((tm, D_MODEL), row)] * 2,
        out_shape=[out, out],
        compiler_params=_cparams(("parallel",)),
        name="outproj",
    )(h2d, mix, w_out_bf, norm_w.reshape(1, D_MODEL))


def _keymat_kernel(sk_ref, wq_ref, m_ref):
    m = lax.dot_general(sk_ref[0], wq_ref[...], (((1,), (1,)), ((), ())),
                        precision=lax.Precision.HIGHEST, preferred_element_type=F32)
    m_ref[...] = m.astype(BF16)


def _keymat(subkeys, w_query):
    nhp = PEER_HEADS * 2
    sk = subkeys.reshape(nhp, N_KEYS, N_KEYS)
    return pl.pallas_call(
        _keymat_kernel,
        grid=(nhp,),
        in_specs=[
            pl.BlockSpec((1, N_KEYS, N_KEYS), lambda i: (i, 0, 0)),
            pl.BlockSpec((D_MODEL, N_KEYS), lambda i: (0, i)),
        ],
        out_specs=pl.BlockSpec((N_KEYS, D_MODEL), lambda i: (i, 0)),
        out_shape=jax.ShapeDtypeStruct((nhp * N_KEYS, D_MODEL), BF16),
        compiler_params=_cparams(("parallel",)),
        name="keymat",
    )(sk, w_query)


_STAIR = [(i, j) for i in range(PEER_TOPK) for j in range(PEER_TOPK) if (i + 1) * (j + 1) <= PEER_TOPK]


def _topk_kernel(xn_ref, m_ref, exp_ref, gate_ref, sc_scr, sv_scr, si_scr):
    tb = xn_ref.shape[0]
    xb = xn_ref[...].astype(BF16)
    sc_scr[...] = lax.dot_general(m_ref[...], xb, (((1,), (1,)), ((), ())), preferred_element_type=F32)
    key_i = lax.broadcasted_iota(jnp.int32, (N_KEYS, tb), 0).astype(F32)

    def head_body(h, carry):
        ss = [sc_scr[pl.ds(pl.multiple_of((2 * h + p) * N_KEYS, N_KEYS), N_KEYS), :] for p in range(2)]
        for i in range(PEER_TOPK):
            for p in range(2):
                s = ss[p]
                mx = jnp.max(s, axis=0, keepdims=True)
                idx = jnp.min(jnp.where(s == mx, key_i, float(N_KEYS)), axis=0, keepdims=True)
                ss[p] = jnp.where(key_i == idx, -jnp.inf, s)
                sv_scr[p, i, pl.ds(h, 1), :] = mx
                si_scr[p, i, pl.ds(h, 1), :] = idx
        return carry

    lax.fori_loop(0, PEER_HEADS, head_body, 0)

    sv0 = [sv_scr[0, i] for i in range(PEER_TOPK)]
    sv1 = [sv_scr[1, i] for i in range(PEER_TOPK)]
    si0 = [si_scr[0, i] for i in range(PEER_TOPK)]
    si1 = [si_scr[1, i] for i in range(PEER_TOPK)]
    cands = [sv0[i] + sv1[j] for (i, j) in _STAIR]
    flats = [float(i * PEER_TOPK + j) for (i, j) in _STAIR]
    tops, experts = [], []
    for r in range(PEER_TOPK):
        mx = functools.reduce(jnp.maximum, cands)
        sel = functools.reduce(jnp.minimum, [jnp.where(cv == mx, fl, 1e9) for cv, fl in zip(cands, flats)])
        cands = [jnp.where(sel == fl, -jnp.inf, cv) for cv, fl in zip(cands, flats)]
        fi = jnp.floor(sel * (1.0 / PEER_TOPK))
        fj = sel - fi * PEER_TOPK
        e0 = functools.reduce(jnp.add, [jnp.where(fi == float(i), si0[i], 0.0) for i in range(PEER_TOPK)])
        e1 = functools.reduce(jnp.add, [jnp.where(fj == float(j), si1[j], 0.0) for j in range(PEER_TOPK)])
        tops.append(mx)
        experts.append((e0 * N_KEYS + e1).astype(jnp.int32))
    ex = [jnp.exp(t - tops[0]) for t in tops]
    den = functools.reduce(jnp.add, ex)
    for r in range(PEER_TOPK):
        exp_ref[0, r] = experts[r]
        gate_ref[0, r] = ex[r] / den


def _topk(xn, keymat):
    n = xn.shape[0]
    tb = TB_TOPK
    assert n % tb == 0
    nb = n // tb
    shp = (nb, PEER_TOPK, PEER_HEADS, tb)
    ospec = pl.BlockSpec((1, PEER_TOPK, PEER_HEADS, tb), lambda i: (i, 0, 0, 0))
    return pl.pallas_call(
        _topk_kernel,
        grid=(nb,),
        in_specs=[
            pl.BlockSpec((tb, D_MODEL), lambda i: (i, 0)),
            pl.BlockSpec(keymat.shape, lambda i: (0, 0)),
        ],
        out_specs=[ospec, ospec],
        out_shape=[jax.ShapeDtypeStruct(shp, jnp.int32), jax.ShapeDtypeStruct(shp, F32)],
        scratch_shapes=[
            pltpu.VMEM((2 * PEER_HEADS * N_KEYS, tb), F32),
            pltpu.VMEM((2, PEER_TOPK, PEER_HEADS, tb), F32),
            pltpu.VMEM((2, PEER_TOPK, PEER_HEADS, tb), F32),
        ],
        compiler_params=_cparams(("parallel",)),
        name="topk",
    )(xn, keymat)


ROW_TILES = 2 * D_MODEL // 128
HALF_TILES = D_MODEL // 128


SUB = 8


def _as_tile(row):
    return jnp.concatenate([row[:, c * 128:(c + 1) * 128] for c in range(HALF_TILES)], axis=0)


def _as_row(tile):
    return jnp.concatenate([tile[c:c + 1, :] for c in range(HALF_TILES)], axis=1)


def _peer_kernel(first_ref, ahead_ref, gate_ref, xn_ref, h1_ref, nf_ref, gsum_ref, tab_ref, y_ref,
                 buf, w_scr, sem):
    tb = xn_ref.shape[0]
    step = pl.program_id(0)

    def start_row(row_ref, j, slot, e):
        pltpu.make_async_copy(tab_ref.at[row_ref[j, e]], buf.at[slot, e], sem.at[slot]).start(priority=e % 2)

    def gather_wait(slot):
        pltpu.make_async_copy(tab_ref.at[pl.ds(0, HITS)], buf.at[slot], sem.at[slot]).wait()

    @pl.when(step == 0)
    def _():
        for t in range(AHEAD):
            for e in range(HITS):
                start_row(first_ref, t, t, e)

    eye = (lax.broadcasted_iota(jnp.int32, (HITS, HITS), 0)
           == lax.broadcasted_iota(jnp.int32, (HITS, HITS), 1)).astype(F32)
    nf = _as_tile(nf_ref[...])
    gs = gsum_ref[...]

    def group_body(g, carry):
        base = pl.multiple_of(g * SUB, SUB)
        cur = (g % 2) * SUB
        nxt = SUB - cur
        x8 = xn_ref[pl.ds(base, SUB), :]
        h8 = h1_ref[pl.ds(base, SUB), :]
        g8 = gate_ref[pl.ds(base, SUB), :]

        for i in range(SUB):
            gather_wait(cur + i)
        parts = []
        for i in range(SUB):
            x3 = _as_tile(x8[i:i + 1, :])
            chunks = []
            for k in range(HITS // CHUNK_HITS):
                prods = []
                for e in range(k * CHUNK_HITS, (k + 1) * CHUNK_HITS):
                    start_row(ahead_ref, base + i, nxt + i, e)
                    prods.append(buf[cur + i, e].astype(F32)[0:HALF_TILES, :] * x3)
                pk = jnp.concatenate(prods, axis=0).astype(BF16)
                chunks.append(jnp.dot(gs, pk, preferred_element_type=F32))
            parts.append(jnp.concatenate(chunks, axis=0))

        for i in range(SUB):
            a_col = jnp.sum(parts[i], axis=1, keepdims=True)
            g_col = jnp.sum(eye * g8[i:i + 1, :], axis=1, keepdims=True)
            w_col = g_col * (0.5 * a_col * (1.0 + lax.erf(a_col * (2.0 ** -0.5))))
            w_scr[i] = jnp.broadcast_to(w_col, (HITS, 128))

        rows = []
        for i in range(SUB):
            accs = [jnp.zeros((HALF_TILES, 128), F32) for _ in range(4)]
            for e in range(HITS):
                v_e = buf[cur + i, e].astype(F32)[HALF_TILES:ROW_TILES, :]
                accs[e % 4] = accs[e % 4] + w_scr[i, pl.ds(e, 1), :] * v_e
            y3 = _as_tile(h8[i:i + 1, :]) + ((accs[0] + accs[1]) + (accs[2] + accs[3]))
            ms = jnp.sum(jnp.sum(y3 * y3, axis=1, keepdims=True), axis=0, keepdims=True) * (1.0 / D_MODEL)
            rows.append(_as_row(y3 * lax.rsqrt(ms + EPS) * nf))
        y_ref[pl.ds(base, SUB), :] = jnp.concatenate(rows, axis=0)
        return carry

    lax.fori_loop(0, tb // SUB, group_body, 0)

    @pl.when(step == pl.num_programs(0) - 1)
    def _():
        for t in range(AHEAD):
            gather_wait(t)


def _peer(idx, gates, xn, h1, norm_final, table):
    n = xn.shape[0]
    tb = TB_PEER
    assert n % (2 * SUB) == 0 and n % tb == 0 and tb % (2 * SUB) == 0
    assert AHEAD == SUB and N_SLOTS == 2 * SUB and HITS % CHUNK_HITS == 0
    r = jnp.arange(CHUNK_HITS * HALF_TILES, dtype=jnp.int32)[None, :] // HALF_TILES
    gsum = (r == jnp.arange(CHUNK_HITS, dtype=jnp.int32)[:, None]).astype(BF16)
    ahead = jnp.concatenate([idx[AHEAD:], jnp.broadcast_to(idx[n - 1:], (AHEAD, HITS))], axis=0)
    first = idx[:SUB]
    row = lambda i: (i, 0)
    const = lambda i: (0, 0)
    return pl.pallas_call(
        _peer_kernel,
        grid=(n // tb,),
        in_specs=[
            pl.BlockSpec((SUB, HITS), const, memory_space=pltpu.SMEM),
            pl.BlockSpec((tb, HITS), row, memory_space=pltpu.SMEM),
            pl.BlockSpec((tb, HITS), row),
            pl.BlockSpec((tb, D_MODEL), row),
            pl.BlockSpec((tb, D_MODEL), row),
            pl.BlockSpec((1, D_MODEL), const),
            pl.BlockSpec(gsum.shape, const),
            pl.BlockSpec(memory_space=pl.ANY),
        ],
        out_specs=pl.BlockSpec((tb, D_MODEL), row),
        out_shape=jax.ShapeDtypeStruct((n, D_MODEL), F32),
        scratch_shapes=[
            pltpu.VMEM((N_SLOTS, HITS, ROW_TILES, 128), table.dtype),
            pltpu.VMEM((SUB, HITS, 128), F32),
            pltpu.SemaphoreType.DMA((N_SLOTS,)),
        ],
        compiler_params=_cparams(("arbitrary",)),
        name="peer",
    )(first, ahead, gates, xn, h1, norm_final.reshape(1, D_MODEL), gsum, table)


def _token_major(a):
    nb, r, h, t = a.shape
    return jnp.transpose(a, (0, 3, 2, 1)).reshape(nb * t, h * r)


def _stream(h_pad, h_main, s0, hist, n_hist, pad, skip, c, consts):
    (norm_mix, w_in_bf, lb_logits, out_norm, pool_w_bf, pool_scale, w_out_bf, norm_ffn,
     keymat, table, norm_final) = consts
    batch, t_len, _ = h_pad.shape
    acts = _inproj(h_pad.reshape(batch * t_len, D_MODEL), norm_mix, w_in_bf, lb_logits)
    mix, s_new = _mixer(acts, s0, hist, out_norm, pool_w_bf, pool_scale,
                        batch=batch, t_len=t_len, c=c, pad=pad, n_hist=n_hist, skip=skip)
    zp = acts[5].reshape(batch, t_len, D_POOL)
    new_hist = zp[:, t_len - POOL_HIST:, :]
    hm = h_main.reshape(-1, D_MODEL)
    h1, xn = _outproj(hm, mix, w_out_bf, norm_ffn)
    experts, gates = _topk(xn, keymat)
    y = _peer(_token_major(experts), _token_major(gates), xn, h1, norm_final, table)
    return y.reshape(h_main.shape), s_new, new_hist


def kernel(x_prompt, x_sample, state_hgrn, cache_pool, meta_tokens, norm_mix, w_in, hgrn_lb_logits,
           hgrn_out_norm, pool_w, pool_scale, w_out, norm_ffn, peer_w_query, peer_subkeys, peer_u,
           peer_v, norm_final):
    assert state_hgrn.shape[0] == 1, "single trunk layer"
    bp, seq, _ = x_prompt.shape
    bs, dseq, _ = x_sample.shape
    dt = x_prompt.dtype

    keymat = _keymat(peer_subkeys[0], peer_w_query[0])
    table = jnp.concatenate([peer_u[0], peer_v[0]], axis=1).astype(BF16).reshape(N_EXPERTS, ROW_TILES, 128)
    consts = (norm_mix[0], w_in[0].astype(BF16), hgrn_lb_logits, hgrn_out_norm[0],
              pool_w[0].astype(BF16), pool_scale[0], w_out[0].astype(BF16), norm_ffn[0],
              keymat, table, norm_final)

    pad = CHUNK - N_META
    head = jnp.concatenate([jnp.zeros((bp, pad, D_MODEL), dt),
                            jnp.broadcast_to(meta_tokens[None].astype(dt), (bp, N_META, D_MODEL))], axis=1)
    hp = jnp.concatenate([head, x_prompt], axis=1)
    s_p0 = jnp.zeros((bp, H_A, DK, DV), F32)
    hist_p0 = jnp.zeros((bp, HIST_ROWS, D_POOL), dt)
    y_p, s_p, c_p = _stream(hp, x_prompt, s_p0, hist_p0, 0, pad, 1, CHUNK, consts)

    hist_s0 = jnp.concatenate([jnp.zeros((bs, HIST_ROWS - POOL_HIST, D_POOL), dt), cache_pool[0]], axis=1)
    y_s, s_s, c_s = _stream(x_sample, x_sample, state_hgrn[0], hist_s0, POOL_HIST, 0, 0, dseq, consts)

    return (y_p, y_s, s_p[None].astype(dt), c_p[None], s_s[None].astype(dt), c_s[None])
```

```python
import functools

import jax
import jax.numpy as jnp
from jax import lax
from jax.experimental import pallas as pl
from jax.experimental.pallas import tpu as pltpu

F32 = jnp.float32
BF16 = jnp.bfloat16

D_MODEL = 1024
N_META = 16
D_HGRN = 512
D_POOL = 512
H_A = 4
DK = 128
DV = 128
POOL_WINDOWS = (2, 4, 8, 16)
POOL_GW = 128
POOL_HIST = 15
D_IN_PROJ = 4 * D_HGRN + D_POOL
PEER_HEADS = 8
N_KEYS = 128
N_EXPERTS = N_KEYS * N_KEYS
PEER_TOPK = 16
EPS = 1e-6
CHUNK = 64
HIST_ROWS = 16
BASE = 8

VMEM_LIMIT_BYTES = 48 * 1024 * 1024

TM_PROJ = 256
TB_TOPK = 128
TB_PEER = 64
N_SLOTS = 24
AHEAD = 8
CHUNK_HITS = 16
HITS = PEER_HEADS * PEER_TOPK
NEG_BIG = -1e30


def _cparams(sem):
    return pltpu.CompilerParams(dimension_semantics=sem, vmem_limit_bytes=VMEM_LIMIT_BYTES)


def _inproj_kernel(h_ref, nw_ref, win_ref, lbl_ref, q_ref, k_ref, lf_ref, v_ref, g_ref, zp_ref):
    x = h_ref[...]
    ms = jnp.mean(x * x, axis=-1, keepdims=True)
    xn = x * lax.rsqrt(ms + EPS) * nw_ref[...]
    proj = jnp.dot(xn.astype(BF16), win_ref[...], preferred_element_type=F32)
    zq = proj[:, 0:D_HGRN]
    zf = proj[:, D_HGRN:2 * D_HGRN]
    zi = proj[:, 2 * D_HGRN:3 * D_HGRN]
    zg = proj[:, 3 * D_HGRN:4 * D_HGRN]
    zp = proj[:, 4 * D_HGRN:]
    lg = lbl_ref[...]
    e = jnp.exp(lg - jnp.max(lg, axis=0, keepdims=True))
    lb = e[0:1, :] / jnp.sum(e, axis=0, keepdims=True)
    q_ref[...] = jax.nn.silu(zq) * (DK ** -0.5)
    lf_ref[...] = jnp.log(lb + (1.0 - lb) * jax.nn.sigmoid(zf))
    k_ref[...] = (1.0 - lb) * jax.nn.sigmoid(-zf)
    v_ref[...] = zi
    g_ref[...] = jax.nn.silu(zg)
    zp_ref[...] = zp


def _inproj(h2d, norm_w, w_in_bf, lb_logits):
    n = h2d.shape[0]
    tm = TM_PROJ
    assert n % tm == 0
    row = lambda i: (i, 0)
    const = lambda i: (0, 0)
    out = jax.ShapeDtypeStruct((n, D_HGRN), F32)
    return pl.pallas_call(
        _inproj_kernel,
        grid=(n // tm,),
        in_specs=[
            pl.BlockSpec((tm, D_MODEL), row),
            pl.BlockSpec((1, D_MODEL), const),
            pl.BlockSpec((D_MODEL, D_IN_PROJ), const),
            pl.BlockSpec(lb_logits.shape, const),
        ],
        out_specs=[pl.BlockSpec((tm, D_HGRN), row)] * 6,
        out_shape=[out] * 6,
        compiler_params=_cparams(("parallel",)),
        name="inproj",
    )(h2d, norm_w.reshape(1, D_MODEL), w_in_bf, lb_logits)


def _levels(c):
    out, m = [], BASE
    while m < c:
        out.append(m)
        m *= 2
    return out


def _blk(x, m):
    return jnp.right_shift(x, m.bit_length() - 1)


def _decay_sum_matrix(c):
    t = lax.broadcasted_iota(jnp.int32, (c, c), 0)
    r = lax.broadcasted_iota(jnp.int32, (c, c), 1)
    mats = [r <= t]
    for m in _levels(c):
        lo = _blk(t, m) * m
        mats.append((r >= lo) & (r <= t))
        mats.append((r > t) & (r <= lo + m - 1))
    return jnp.concatenate([x.astype(F32) for x in mats], axis=0).astype(BF16)


def _mixer_kernel(q_ref, k_ref, lf_ref, v_ref, g_ref, zp_ref, s0_ref, hist_ref, onorm_ref,
                  pw_ref, ps_ref, mix_ref, sout_ref, st_scr, prev_scr, *, c, pad, n_hist):
    ci = pl.program_id(1)
    nc = pl.num_programs(1)

    @pl.when(ci == 0)
    def _():
        for hd in range(H_A):
            st_scr[hd] = s0_ref[0, hd].T
        prev_scr[...] = hist_ref[0]

    row = ci * c + lax.broadcasted_iota(jnp.int32, (c, 1), 0)
    live = row >= pad
    lf_all = jnp.where(live, lf_ref[...], 0.0)
    k_all = jnp.where(live, k_ref[...], 0.0)
    q_all = q_ref[...]
    v_all = v_ref[...]

    wmat = _decay_sum_matrix(c)
    lf_hi = lf_all.astype(BF16)
    r1 = lf_all - lf_hi.astype(F32)
    lf_mid = r1.astype(BF16)
    lf_lo = (r1 - lf_mid.astype(F32)).astype(BF16)
    sums = (jnp.dot(wmat, lf_hi, preferred_element_type=F32)
            + jnp.dot(wmat, lf_mid, preferred_element_type=F32)
            + jnp.dot(wmat, lf_lo, preferred_element_type=F32))

    levels = _levels(c)
    t_i = lax.broadcasted_iota(jnp.int32, (c, c), 0)
    s_i = lax.broadcasted_iota(jnp.int32, (c, c), 1)
    row_i = lax.broadcasted_iota(jnp.int32, (c, 1), 0)
    sub_i = lax.broadcasted_iota(jnp.int32, (c // BASE, BASE, 1), 1)
    ones_b = jnp.ones((DK, DV), BF16)
    onorm = onorm_ref[...]
    g_all = g_ref[...]

    o_heads = []
    for hd in range(H_A):
        sl = slice(hd * DK, (hd + 1) * DK)
        q = q_all[:, sl]
        k = k_all[:, sl]
        v = v_all[:, sl]
        b = sums[0:c, sl]
        st = st_scr[hd]

        o = lax.dot_general((q * jnp.exp(b)).astype(BF16), st.astype(BF16),
                            (((1,), (1,)), ((), ())), preferred_element_type=F32)

        a_mat = jnp.zeros((c, c), F32)
        for li, m in enumerate(levels):
            eq = sums[(1 + 2 * li) * c:(2 + 2 * li) * c, sl]
            ek = sums[(2 + 2 * li) * c:(3 + 2 * li) * c, sl]
            odd = (_blk(row_i, m) & 1) == 1
            qs = q * jnp.exp(jnp.where(odd, eq, NEG_BIG))
            ks = k * jnp.exp(jnp.where(odd, NEG_BIG, ek))
            am = lax.dot_general(qs.astype(BF16), ks.astype(BF16),
                                 (((1,), (1,)), ((), ())), preferred_element_type=F32)
            pair = ((_blk(t_i, 2 * m) == _blk(s_i, 2 * m))
                    & ((_blk(t_i, m) & 1) == 1) & ((_blk(s_i, m) & 1) == 0))
            a_mat = a_mat + jnp.where(pair, am, 0.0)
        o = o + jnp.dot(a_mat.astype(BF16), v.astype(BF16), preferred_element_type=F32)

        q3 = q.reshape(c // BASE, BASE, DK)
        k3 = k.reshape(c // BASE, BASE, DK)
        v3 = v.reshape(c // BASE, BASE, DV)
        b3 = b.reshape(c // BASE, BASE, DK)
        prods = []
        for d in range(BASE):
            if d == 0:
                prods.append(q3 * k3)
            else:
                kd = pltpu.roll(k3, d, 1)
                bd = pltpu.roll(b3, d, 1)
                dec = jnp.exp(jnp.where(sub_i >= d, b3 - bd, NEG_BIG))
                prods.append(q3 * kd * dec)
        pst = jnp.concatenate(prods, axis=0).reshape(BASE * c, DK)
        asum = jnp.dot(pst.astype(BF16), ones_b, preferred_element_type=F32)
        for d in range(BASE):
            vd = v if d == 0 else pltpu.roll(v3, d, 1).reshape(c, DV)
            o = o + asum[d * c:(d + 1) * c, :] * vd

        bl = b[c - 1:c, :]
        kdec = k * jnp.exp(bl - b)
        upd = lax.dot_general(v.astype(BF16), kdec.astype(BF16),
                              (((0,), (0,)), ((), ())), preferred_element_type=F32)
        st_new = st * jnp.exp(bl) + upd
        st_scr[hd] = st_new

        @pl.when(ci == nc - 1)
        def _():
            sout_ref[0, hd] = st_new.T

        o = o * lax.rsqrt(jnp.mean(o * o, axis=-1, keepdims=True) + EPS)
        o_heads.append(o * onorm[:, sl] * g_all[:, sl])

    zp = zp_ref[...]
    ext = jnp.concatenate([prev_scr[...], zp], axis=0)
    tok = (row - pad).astype(F32)
    avail = jnp.maximum(n_hist + tok + 1.0, 1.0)
    pool_heads = []
    for gi, w in enumerate(POOL_WINDOWS):
        sl = slice(gi * POOL_GW, (gi + 1) * POOL_GW)
        s = ext[:, sl]
        sh = 1
        while sh < w:
            s = s + pltpu.roll(s, sh, 0)
            sh *= 2
        win = s[HIST_ROWS:, :]
        cnt = jnp.minimum(float(w), avail)
        dlt = win / cnt - zp[:, sl]
        po = jnp.dot(dlt.astype(BF16), pw_ref[gi], preferred_element_type=F32)
        pool_heads.append(po * ps_ref[:, sl])
    prev_scr[...] = zp[c - HIST_ROWS:, :]

    mix_ref[...] = jnp.concatenate(o_heads + pool_heads, axis=1).astype(BF16)


def _mixer(acts, s0, hist, out_norm, pool_w_bf, pool_scale, *, batch, t_len, c, pad, n_hist, skip):
    nc = t_len // c
    assert nc * c == t_len and c % BASE == 0 and c >= HIST_ROWS
    blk = lambda b, ci: (b * nc + ci, 0)
    per_b4 = lambda b, ci: (b, 0, 0, 0)
    per_b3 = lambda b, ci: (b, 0, 0)
    c2 = lambda b, ci: (0, 0)
    c3 = lambda b, ci: (0, 0, 0)
    n_main = nc - skip
    mix_map = lambda b, ci: (b * n_main + jnp.maximum(ci - skip, 0), 0)
    kern = functools.partial(_mixer_kernel, c=c, pad=pad, n_hist=float(n_hist))
    return pl.pallas_call(
        kern,
        grid=(batch, nc),
        in_specs=[pl.BlockSpec((c, D_HGRN), blk)] * 6 + [
            pl.BlockSpec((1, H_A, DK, DV), per_b4),
            pl.BlockSpec((1, HIST_ROWS, D_POOL), per_b3),
            pl.BlockSpec((1, D_HGRN), c2),
            pl.BlockSpec((len(POOL_WINDOWS), POOL_GW, POOL_GW), c3),
            pl.BlockSpec((1, D_POOL), c2),
        ],
        out_specs=[
            pl.BlockSpec((c, D_MODEL), mix_map),
            pl.BlockSpec((1, H_A, DK, DV), per_b4),
        ],
        out_shape=[
            jax.ShapeDtypeStruct((batch * n_main * c, D_MODEL), BF16),
            jax.ShapeDtypeStruct((batch, H_A, DK, DV), F32),
        ],
        scratch_shapes=[
            pltpu.VMEM((H_A, DV, DK), F32),
            pltpu.VMEM((HIST_ROWS, D_POOL), F32),
        ],
        compiler_params=_cparams(("parallel", "arbitrary")),
        name="mixer",
    )(*acts, s0, hist, out_norm.reshape(1, D_HGRN), pool_w_bf, pool_scale.reshape(1, D_POOL))


def _outproj_kernel(h_ref, mix_ref, wout_ref, nw_ref, h1_ref, xn_ref):
    h1 = h_ref[...] + jnp.dot(mix_ref[...], wout_ref[...], preferred_element_type=F32)
    h1_ref[...] = h1
    ms = jnp.mean(h1 * h1, axis=-1, keepdims=True)
    xn_ref[...] = h1 * lax.rsqrt(ms + EPS) * nw_ref[...]


def _outproj(h2d, mix, w_out_bf, norm_w):
    n = h2d.shape[0]
    tm = TM_PROJ
    assert n % tm == 0
    row = lambda i: (i, 0)
    const = lambda i: (0, 0)
    out = jax.ShapeDtypeStruct((n, D_MODEL), F32)
    return pl.pallas_call(
        _outproj_kernel,
        grid=(n // tm,),
        in_specs=[
            pl.BlockSpec((tm, D_MODEL), row),
            pl.BlockSpec((tm, D_MODEL), row),
            pl.BlockSpec((D_MODEL, D_MODEL), const),
            pl.BlockSpec((1, D_MODEL), const),
        ],
        out_specs=[pl.BlockSpec((tm, D_MODEL), row)] * 2,
        out_shape=[out, out],
        compiler_params=_cparams(("parallel",)),
        name="outproj",
    )(h2d, mix, w_out_bf, norm_w.reshape(1, D_MODEL))


def _keymat_kernel(sk_ref, wq_ref, m_ref):
    m = lax.dot_general(sk_ref[0], wq_ref[...], (((1,), (1,)), ((), ())),
                        precision=lax.Precision.HIGHEST, preferred_element_type=F32)
    m_ref[...] = m.astype(BF16)


def _keymat(subkeys, w_query):
    nhp = PEER_HEADS * 2
    sk = subkeys.reshape(nhp, N_KEYS, N_KEYS)
    return pl.pallas_call(
        _keymat_kernel,
        grid=(nhp,),
        in_specs=[
            pl.BlockSpec((1, N_KEYS, N_KEYS), lambda i: (i, 0, 0)),
            pl.BlockSpec((D_MODEL, N_KEYS), lambda i: (0, i)),
        ],
        out_specs=pl.BlockSpec((N_KEYS, D_MODEL), lambda i: (i, 0)),
        out_shape=jax.ShapeDtypeStruct((nhp * N_KEYS, D_MODEL), BF16),
        compiler_params=_cparams(("parallel",)),
        name="keymat",
    )(sk, w_query)


_STAIR = [(i, j) for i in range(PEER_TOPK) for j in range(PEER_TOPK) if (i + 1) * (j + 1) <= PEER_TOPK]


def _topk_kernel(xn_ref, m_ref, exp_ref, gate_ref, sc_scr, sv_scr, si_scr):
    tb = xn_ref.shape[0]
    xb = xn_ref[...].astype(BF16)
    sc_scr[...] = lax.dot_general(m_ref[...], xb, (((1,), (1,)), ((), ())), preferred_element_type=F32)
    key_i = lax.broadcasted_iota(jnp.int32, (N_KEYS, tb), 0).astype(F32)

    def head_body(h, carry):
        ss = [sc_scr[pl.ds(pl.multiple_of((2 * h + p) * N_KEYS, N_KEYS), N_KEYS), :] for p in range(2)]
        for i in range(PEER_TOPK):
            for p in range(2):
                s = ss[p]
                mx = jnp.max(s, axis=0, keepdims=True)
                idx = jnp.min(jnp.where(s == mx, key_i, float(N_KEYS)), axis=0, keepdims=True)
                ss[p] = jnp.where(key_i == idx, -jnp.inf, s)
                sv_scr[p, i, pl.ds(h, 1), :] = mx
                si_scr[p, i, pl.ds(h, 1), :] = idx
        return carry

    lax.fori_loop(0, PEER_HEADS, head_body, 0)

    sv0 = [sv_scr[0, i] for i in range(PEER_TOPK)]
    sv1 = [sv_scr[1, i] for i in range(PEER_TOPK)]
    si0 = [si_scr[0, i] for i in range(PEER_TOPK)]
    si1 = [si_scr[1, i] for i in range(PEER_TOPK)]
    cands = [sv0[i] + sv1[j] for (i, j) in _STAIR]
    flats = [float(i * PEER_TOPK + j) for (i, j) in _STAIR]
    tops, experts = [], []
    for r in range(PEER_TOPK):
        mx = functools.reduce(jnp.maximum, cands)
        sel = functools.reduce(jnp.minimum, [jnp.where(cv == mx, fl, 1e9) for cv, fl in zip(cands, flats)])
        cands = [jnp.where(sel == fl, -jnp.inf, cv) for cv, fl in zip(cands, flats)]
        fi = jnp.floor(sel * (1.0 / PEER_TOPK))
        fj = sel - fi * PEER_TOPK
        e0 = functools.reduce(jnp.add, [jnp.where(fi == float(i), si0[i], 0.0) for i in range(PEER_TOPK)])
        e1 = functools.reduce(jnp.add, [jnp.where(fj == float(j), si1[j], 0.0) for j in range(PEER_TOPK)])
        tops.append(mx)
        experts.append((e0 * N_KEYS + e1).astype(jnp.int32))
    ex = [jnp.exp(t - tops[0]) for t in tops]
    den = functools.reduce(jnp.add, ex)
    for r in range(PEER_TOPK):
        exp_ref[0, r] = experts[r]
        gate_ref[0, r] = ex[r] / den


def _topk(xn, keymat):
    n = xn.shape[0]
    tb = TB_TOPK
    assert n % tb == 0
    nb = n // tb
    shp = (nb, PEER_TOPK, PEER_HEADS, tb)
    ospec = pl.BlockSpec((1, PEER_TOPK, PEER_HEADS, tb), lambda i: (i, 0, 0, 0))
    return pl.pallas_call(
        _topk_kernel,
        grid=(nb,),
        in_specs=[
            pl.BlockSpec((tb, D_MODEL), lambda i: (i, 0)),
            pl.BlockSpec(keymat.shape, lambda i: (0, 0)),
        ],
        out_specs=[ospec, ospec],
        out_shape=[jax.ShapeDtypeStruct(shp, jnp.int32), jax.ShapeDtypeStruct(shp, F32)],
        scratch_shapes=[
            pltpu.VMEM((2 * PEER_HEADS * N_KEYS, tb), F32),
            pltpu.VMEM((2, PEER_TOPK, PEER_HEADS, tb), F32),
            pltpu.VMEM((2, PEER_TOPK, PEER_HEADS, tb), F32),
        ],
        compiler_params=_cparams(("parallel",)),
        name="topk",
    )(xn, keymat)


ROW_TILES = 2 * D_MODEL // 128
HALF_TILES = D_MODEL // 128


SUB = 8


def _as_tile(row):
    return jnp.concatenate([row[:, c * 128:(c + 1) * 128] for c in range(HALF_TILES)], axis=0)


def _as_row(tile):
    return jnp.concatenate([tile[c:c + 1, :] for c in range(HALF_TILES)], axis=1)


def _peer_kernel(first_ref, ahead_ref, gate_ref, xn_ref, h1_ref, nf_ref, gsum_ref, tab_ref, y_ref,
                 buf, w_scr, sem, *, n_groups):
    tb = xn_ref.shape[0]
    gps = tb // SUB
    step = pl.program_id(0)

    def start_row(row_ref, j, slot, e):
        pltpu.make_async_copy(tab_ref.at[row_ref[j, e]], buf.at[slot, e], sem.at[slot]).start(priority=e % 2)

    def gather_wait(slot):
        pltpu.make_async_copy(tab_ref.at[pl.ds(0, HITS)], buf.at[slot], sem.at[slot]).wait()

    @pl.when(step == 0)
    def _():
        for t in range(SUB):
            for e in range(HITS):
                start_row(first_ref, t, t, e)

    eye = (lax.broadcasted_iota(jnp.int32, (HITS, HITS), 0)
           == lax.broadcasted_iota(jnp.int32, (HITS, HITS), 1)).astype(F32)
    nf = _as_tile(nf_ref[...])
    gs = gsum_ref[...]

    def v_row(v_slot, w_half, i, e):
        return w_scr[w_half, i, pl.ds(e, 1), :] * buf[v_slot + i, e].astype(F32)[HALF_TILES:ROW_TILES, :]

    def finish(accs, h_row):
        y3 = _as_tile(h_row) + ((accs[0] + accs[1]) + (accs[2] + accs[3]))
        ms = jnp.sum(jnp.sum(y3 * y3, axis=1, keepdims=True), axis=0, keepdims=True) * (1.0 / D_MODEL)
        return _as_row(y3 * lax.rsqrt(ms + EPS) * nf)

    def group(g, with_prev):
        base = pl.multiple_of(g * SUB, SUB) if with_prev else 0
        gg = step * gps + g
        cur = lax.rem(gg, 3) * SUB
        nxt = lax.rem(gg + 1, 3) * SUB
        prv = lax.rem(gg + 2, 3) * SUB
        w_cur = lax.rem(gg, 2)
        x8 = xn_ref[pl.ds(base, SUB), :]
        g8 = gate_ref[pl.ds(base, SUB), :]
        if with_prev:
            hp8 = h1_ref[pl.ds(base - SUB, SUB), :]

        for i in range(SUB):
            gather_wait(cur + i)
        parts, rows = [], []
        for i in range(SUB):
            x3 = _as_tile(x8[i:i + 1, :])
            accs = [jnp.zeros((HALF_TILES, 128), F32) for _ in range(4)]
            chunks = []
            for k in range(HITS // CHUNK_HITS):
                prods = []
                for e in range(k * CHUNK_HITS, (k + 1) * CHUNK_HITS):
                    start_row(ahead_ref, base + i, nxt + i, e)
                    prods.append(buf[cur + i, e].astype(F32)[0:HALF_TILES, :] * x3)
                    if with_prev:
                        accs[e % 4] = accs[e % 4] + v_row(prv, 1 - w_cur, i, e)
                pk = jnp.concatenate(prods, axis=0).astype(BF16)
                chunks.append(jnp.dot(gs, pk, preferred_element_type=F32))
            parts.append(jnp.concatenate(chunks, axis=0))
            if with_prev:
                rows.append(finish(accs, hp8[i:i + 1, :]))
        if with_prev:
            y_ref[pl.ds(base - SUB, SUB), :] = jnp.concatenate(rows, axis=0)

        for i in range(SUB):
            a_col = jnp.sum(parts[i], axis=1, keepdims=True)
            g_col = jnp.sum(eye * g8[i:i + 1, :], axis=1, keepdims=True)
            w_col = g_col * (0.5 * a_col * (1.0 + lax.erf(a_col * (2.0 ** -0.5))))
            w_scr[w_cur, i] = jnp.broadcast_to(w_col, (HITS, 128))

    group(0, False)

    def loop_body(g, carry):
        group(g, True)
        return carry

    lax.fori_loop(1, gps, loop_body, 0)

    gg = step * gps + (gps - 1)
    cur = lax.rem(gg, 3) * SUB
    w_cur = lax.rem(gg, 2)
    h8 = h1_ref[pl.ds(tb - SUB, SUB), :]
    rows = []
    for i in range(SUB):
        accs = [jnp.zeros((HALF_TILES, 128), F32) for _ in range(4)]
        for e in range(HITS):
            accs[e % 4] = accs[e % 4] + v_row(cur, w_cur, i, e)
        rows.append(finish(accs, h8[i:i + 1, :]))
    y_ref[pl.ds(tb - SUB, SUB), :] = jnp.concatenate(rows, axis=0)

    @pl.when(step == pl.num_programs(0) - 1)
    def _():
        for t in range(SUB):
            gather_wait((n_groups % 3) * SUB + t)


def _peer(idx, gates, xn, h1, norm_final, table):
    n = xn.shape[0]
    tb = TB_PEER
    assert n % tb == 0 and tb % SUB == 0 and tb // SUB >= 2
    assert AHEAD == SUB and N_SLOTS == 3 * SUB and HITS % CHUNK_HITS == 0
    r = jnp.arange(CHUNK_HITS * HALF_TILES, dtype=jnp.int32)[None, :] // HALF_TILES
    gsum = (r == jnp.arange(CHUNK_HITS, dtype=jnp.int32)[:, None]).astype(BF16)
    ahead = jnp.concatenate([idx[AHEAD:], jnp.broadcast_to(idx[n - 1:], (AHEAD, HITS))], axis=0)
    first = idx[:SUB]
    row = lambda i: (i, 0)
    const = lambda i: (0, 0)
    return pl.pallas_call(
        functools.partial(_peer_kernel, n_groups=n // SUB),
        grid=(n // tb,),
        in_specs=[
            pl.BlockSpec((SUB, HITS), const, memory_space=pltpu.SMEM),
            pl.BlockSpec((tb, HITS), row, memory_space=pltpu.SMEM),
            pl.BlockSpec((tb, HITS), row),
            pl.BlockSpec((tb, D_MODEL), row),
            pl.BlockSpec((tb, D_MODEL), row),
            pl.BlockSpec((1, D_MODEL), const),
            pl.BlockSpec(gsum.shape, const),
            pl.BlockSpec(memory_space=pl.ANY),
        ],
        out_specs=pl.BlockSpec((tb, D_MODEL), row),
        out_shape=jax.ShapeDtypeStruct((n, D_MODEL), F32),
        scratch_shapes=[
            pltpu.VMEM((N_SLOTS, HITS, ROW_TILES, 128), table.dtype),
            pltpu.VMEM((2, SUB, HITS, 128), F32),
            pltpu.SemaphoreType.DMA((N_SLOTS,)),
        ],
        compiler_params=_cparams(("arbitrary",)),
        name="peer",
    )(first, ahead, gates, xn, h1, norm_final.reshape(1, D_MODEL), gsum, table)


def _token_major(a):
    nb, r, h, t = a.shape
    return jnp.transpose(a, (0, 3, 2, 1)).reshape(nb * t, h * r)


def _stream(h_pad, h_main, s0, hist, n_hist, pad, skip, c, consts):
    (norm_mix, w_in_bf, lb_logits, out_norm, pool_w_bf, pool_scale, w_out_bf, norm_ffn,
     keymat, table, norm_final) = consts
    batch, t_len, _ = h_pad.shape
    acts = _inproj(h_pad.reshape(batch * t_len, D_MODEL), norm_mix, w_in_bf, lb_logits)
    mix, s_new = _mixer(acts, s0, hist, out_norm, pool_w_bf, pool_scale,
                        batch=batch, t_len=t_len, c=c, pad=pad, n_hist=n_hist, skip=skip)
    zp = acts[5].reshape(batch, t_len, D_POOL)
    new_hist = zp[:, t_len - POOL_HIST:, :]
    hm = h_main.reshape(-1, D_MODEL)
    h1, xn = _outproj(hm, mix, w_out_bf, norm_ffn)
    experts, gates = _topk(xn, keymat)
    y = _peer(_token_major(experts), _token_major(gates), xn, h1, norm_final, table)
    return y.reshape(h_main.shape), s_new, new_hist


def kernel(x_prompt, x_sample, state_hgrn, cache_pool, meta_tokens, norm_mix, w_in, hgrn_lb_logits,
           hgrn_out_norm, pool_w, pool_scale, w_out, norm_ffn, peer_w_query, peer_subkeys, peer_u,
           peer_v, norm_final):
    assert state_hgrn.shape[0] == 1, "single trunk layer"
    bp, seq, _ = x_prompt.shape
    bs, dseq, _ = x_sample.shape
    dt = x_prompt.dtype

    keymat = _keymat(peer_subkeys[0], peer_w_query[0])
    table = jnp.concatenate([peer_u[0], peer_v[0]], axis=1).astype(BF16).reshape(N_EXPERTS, ROW_TILES, 128)
    consts = (norm_mix[0], w_in[0].astype(BF16), hgrn_lb_logits, hgrn_out_norm[0],
              pool_w[0].astype(BF16), pool_scale[0], w_out[0].astype(BF16), norm_ffn[0],
              keymat, table, norm_final)

    pad = CHUNK - N_META
    head = jnp.concatenate([jnp.zeros((bp, pad, D_MODEL), dt),
                            jnp.broadcast_to(meta_tokens[None].astype(dt), (bp, N_META, D_MODEL))], axis=1)
    hp = jnp.concatenate([head, x_prompt], axis=1)
    s_p0 = jnp.zeros((bp, H_A, DK, DV), F32)
    hist_p0 = jnp.zeros((bp, HIST_ROWS, D_POOL), dt)
    y_p, s_p, c_p = _stream(hp, x_prompt, s_p0, hist_p0, 0, pad, 1, CHUNK, consts)

    hist_s0 = jnp.concatenate([jnp.zeros((bs, HIST_ROWS - POOL_HIST, D_POOL), dt), cache_pool[0]], axis=1)
    y_s, s_s, c_s = _stream(x_sample, x_sample, state_hgrn[0], hist_s0, POOL_HIST, 0, 0, dseq, consts)

    return (y_p, y_s, s_p[None].astype(dt), c_p[None], s_s[None].astype(dt), c_s[None])
```

```python
import functools

import jax
import jax.numpy as jnp
from jax import lax
from jax.experimental import pallas as pl
from jax.experimental.pallas import tpu as pltpu
from jax.experimental.pallas import tpu_sc as plsc

F32 = jnp.float32
BF16 = jnp.bfloat16

D_MODEL = 1024
N_META = 16
D_HGRN = 512
D_POOL = 512
H_A = 4
DK = 128
DV = 128
POOL_WINDOWS = (2, 4, 8, 16)
POOL_GW = 128
POOL_HIST = 15
D_IN_PROJ = 4 * D_HGRN + D_POOL
PEER_HEADS = 8
N_KEYS = 128
N_EXPERTS = N_KEYS * N_KEYS
PEER_TOPK = 16
EPS = 1e-6
CHUNK = 64
HIST_ROWS = 16
BASE = 8

VMEM_LIMIT_BYTES = 48 * 1024 * 1024

TM_PROJ = 256
TB_TOPK = 128
TB_PEER = 64
N_SLOTS = 24
AHEAD = 8
CHUNK_HITS = 16
SC_WINDOW = 32
SC_SHARE_PCT = 50
SC_MIN_TOKENS = 4096
HITS = PEER_HEADS * PEER_TOPK
NEG_BIG = -1e30


def _cparams(sem):
    return pltpu.CompilerParams(dimension_semantics=sem, vmem_limit_bytes=VMEM_LIMIT_BYTES)


def _inproj_kernel(h_ref, nw_ref, win_ref, lbl_ref, q_ref, k_ref, lf_ref, v_ref, g_ref, zp_ref):
    x = h_ref[...]
    ms = jnp.mean(x * x, axis=-1, keepdims=True)
    xn = x * lax.rsqrt(ms + EPS) * nw_ref[...]
    proj = jnp.dot(xn.astype(BF16), win_ref[...], preferred_element_type=F32)
    zq = proj[:, 0:D_HGRN]
    zf = proj[:, D_HGRN:2 * D_HGRN]
    zi = proj[:, 2 * D_HGRN:3 * D_HGRN]
    zg = proj[:, 3 * D_HGRN:4 * D_HGRN]
    zp = proj[:, 4 * D_HGRN:]
    lg = lbl_ref[...]
    e = jnp.exp(lg - jnp.max(lg, axis=0, keepdims=True))
    lb = e[0:1, :] / jnp.sum(e, axis=0, keepdims=True)
    q_ref[...] = jax.nn.silu(zq) * (DK ** -0.5)
    lf_ref[...] = jnp.log(lb + (1.0 - lb) * jax.nn.sigmoid(zf))
    k_ref[...] = (1.0 - lb) * jax.nn.sigmoid(-zf)
    v_ref[...] = zi
    g_ref[...] = jax.nn.silu(zg)
    zp_ref[...] = zp


def _inproj(h2d, norm_w, w_in_bf, lb_logits):
    n = h2d.shape[0]
    tm = TM_PROJ
    assert n % tm == 0
    row = lambda i: (i, 0)
    const = lambda i: (0, 0)
    out = jax.ShapeDtypeStruct((n, D_HGRN), F32)
    return pl.pallas_call(
        _inproj_kernel,
        grid=(n // tm,),
        in_specs=[
            pl.BlockSpec((tm, D_MODEL), row),
            pl.BlockSpec((1, D_MODEL), const),
            pl.BlockSpec((D_MODEL, D_IN_PROJ), const),
            pl.BlockSpec(lb_logits.shape, const),
        ],
        out_specs=[pl.BlockSpec((tm, D_HGRN), row)] * 6,
        out_shape=[out] * 6,
        compiler_params=_cparams(("parallel",)),
        name="inproj",
    )(h2d, norm_w.reshape(1, D_MODEL), w_in_bf, lb_logits)


def _levels(c):
    out, m = [], BASE
    while m < c:
        out.append(m)
        m *= 2
    return out


def _blk(x, m):
    return jnp.right_shift(x, m.bit_length() - 1)


def _decay_sum_matrix(c):
    t = lax.broadcasted_iota(jnp.int32, (c, c), 0)
    r = lax.broadcasted_iota(jnp.int32, (c, c), 1)
    mats = [r <= t]
    for m in _levels(c):
        lo = _blk(t, m) * m
        mats.append((r >= lo) & (r <= t))
        mats.append((r > t) & (r <= lo + m - 1))
    return jnp.concatenate([x.astype(F32) for x in mats], axis=0).astype(BF16)


def _mixer_kernel(q_ref, k_ref, lf_ref, v_ref, g_ref, zp_ref, s0_ref, hist_ref, onorm_ref,
                  pw_ref, ps_ref, mix_ref, sout_ref, st_scr, prev_scr, *, c, pad, n_hist):
    ci = pl.program_id(1)
    nc = pl.num_programs(1)

    @pl.when(ci == 0)
    def _():
        for hd in range(H_A):
            st_scr[hd] = s0_ref[0, hd].T
        prev_scr[...] = hist_ref[0]

    row = ci * c + lax.broadcasted_iota(jnp.int32, (c, 1), 0)
    live = row >= pad
    lf_all = jnp.where(live, lf_ref[...], 0.0)
    k_all = jnp.where(live, k_ref[...], 0.0)
    q_all = q_ref[...]
    v_all = v_ref[...]

    wmat = _decay_sum_matrix(c)
    lf_hi = lf_all.astype(BF16)
    r1 = lf_all - lf_hi.astype(F32)
    lf_mid = r1.astype(BF16)
    lf_lo = (r1 - lf_mid.astype(F32)).astype(BF16)
    sums = (jnp.dot(wmat, lf_hi, preferred_element_type=F32)
            + jnp.dot(wmat, lf_mid, preferred_element_type=F32)
            + jnp.dot(wmat, lf_lo, preferred_element_type=F32))

    levels = _levels(c)
    t_i = lax.broadcasted_iota(jnp.int32, (c, c), 0)
    s_i = lax.broadcasted_iota(jnp.int32, (c, c), 1)
    row_i = lax.broadcasted_iota(jnp.int32, (c, 1), 0)
    sub_i = lax.broadcasted_iota(jnp.int32, (c // BASE, BASE, 1), 1)
    ones_b = jnp.ones((DK, DV), BF16)
    onorm = onorm_ref[...]
    g_all = g_ref[...]

    o_heads = []
    for hd in range(H_A):
        sl = slice(hd * DK, (hd + 1) * DK)
        q = q_all[:, sl]
        k = k_all[:, sl]
        v = v_all[:, sl]
        b = sums[0:c, sl]
        st = st_scr[hd]

        o = lax.dot_general((q * jnp.exp(b)).astype(BF16), st.astype(BF16),
                            (((1,), (1,)), ((), ())), preferred_element_type=F32)

        a_mat = jnp.zeros((c, c), F32)
        for li, m in enumerate(levels):
            eq = sums[(1 + 2 * li) * c:(2 + 2 * li) * c, sl]
            ek = sums[(2 + 2 * li) * c:(3 + 2 * li) * c, sl]
            odd = (_blk(row_i, m) & 1) == 1
            qs = q * jnp.exp(jnp.where(odd, eq, NEG_BIG))
            ks = k * jnp.exp(jnp.where(odd, NEG_BIG, ek))
            am = lax.dot_general(qs.astype(BF16), ks.astype(BF16),
                                 (((1,), (1,)), ((), ())), preferred_element_type=F32)
            pair = ((_blk(t_i, 2 * m) == _blk(s_i, 2 * m))
                    & ((_blk(t_i, m) & 1) == 1) & ((_blk(s_i, m) & 1) == 0))
            a_mat = a_mat + jnp.where(pair, am, 0.0)
        o = o + jnp.dot(a_mat.astype(BF16), v.astype(BF16), preferred_element_type=F32)

        q3 = q.reshape(c // BASE, BASE, DK)
        k3 = k.reshape(c // BASE, BASE, DK)
        v3 = v.reshape(c // BASE, BASE, DV)
        b3 = b.reshape(c // BASE, BASE, DK)
        prods = []
        for d in range(BASE):
            if d == 0:
                prods.append(q3 * k3)
            else:
                kd = pltpu.roll(k3, d, 1)
                bd = pltpu.roll(b3, d, 1)
                dec = jnp.exp(jnp.where(sub_i >= d, b3 - bd, NEG_BIG))
                prods.append(q3 * kd * dec)
        pst = jnp.concatenate(prods, axis=0).reshape(BASE * c, DK)
        asum = jnp.dot(pst.astype(BF16), ones_b, preferred_element_type=F32)
        for d in range(BASE):
            vd = v if d == 0 else pltpu.roll(v3, d, 1).reshape(c, DV)
            o = o + asum[d * c:(d + 1) * c, :] * vd

        bl = b[c - 1:c, :]
        kdec = k * jnp.exp(bl - b)
        upd = lax.dot_general(v.astype(BF16), kdec.astype(BF16),
                              (((0,), (0,)), ((), ())), preferred_element_type=F32)
        st_new = st * jnp.exp(bl) + upd
        st_scr[hd] = st_new

        @pl.when(ci == nc - 1)
        def _():
            sout_ref[0, hd] = st_new.T

        o = o * lax.rsqrt(jnp.mean(o * o, axis=-1, keepdims=True) + EPS)
        o_heads.append(o * onorm[:, sl] * g_all[:, sl])

    zp = zp_ref[...]
    ext = jnp.concatenate([prev_scr[...], zp], axis=0)
    tok = (row - pad).astype(F32)
    avail = jnp.maximum(n_hist + tok + 1.0, 1.0)
    pool_heads = []
    for gi, w in enumerate(POOL_WINDOWS):
        sl = slice(gi * POOL_GW, (gi + 1) * POOL_GW)
        s = ext[:, sl]
        sh = 1
        while sh < w:
            s = s + pltpu.roll(s, sh, 0)
            sh *= 2
        win = s[HIST_ROWS:, :]
        cnt = jnp.minimum(float(w), avail)
        dlt = win / cnt - zp[:, sl]
        po = jnp.dot(dlt.astype(BF16), pw_ref[gi], preferred_element_type=F32)
        pool_heads.append(po * ps_ref[:, sl])
    prev_scr[...] = zp[c - HIST_ROWS:, :]

    mix_ref[...] = jnp.concatenate(o_heads + pool_heads, axis=1).astype(BF16)


def _mixer(acts, s0, hist, out_norm, pool_w_bf, pool_scale, *, batch, t_len, c, pad, n_hist, skip):
    nc = t_len // c
    assert nc * c == t_len and c % BASE == 0 and c >= HIST_ROWS
    blk = lambda b, ci: (b * nc + ci, 0)
    per_b4 = lambda b, ci: (b, 0, 0, 0)
    per_b3 = lambda b, ci: (b, 0, 0)
    c2 = lambda b, ci: (0, 0)
    c3 = lambda b, ci: (0, 0, 0)
    n_main = nc - skip
    mix_map = lambda b, ci: (b * n_main + jnp.maximum(ci - skip, 0), 0)
    kern = functools.partial(_mixer_kernel, c=c, pad=pad, n_hist=float(n_hist))
    return pl.pallas_call(
        kern,
        grid=(batch, nc),
        in_specs=[pl.BlockSpec((c, D_HGRN), blk)] * 6 + [
            pl.BlockSpec((1, H_A, DK, DV), per_b4),
            pl.BlockSpec((1, HIST_ROWS, D_POOL), per_b3),
            pl.BlockSpec((1, D_HGRN), c2),
            pl.BlockSpec((len(POOL_WINDOWS), POOL_GW, POOL_GW), c3),
            pl.BlockSpec((1, D_POOL), c2),
        ],
        out_specs=[
            pl.BlockSpec((c, D_MODEL), mix_map),
            pl.BlockSpec((1, H_A, DK, DV), per_b4),
        ],
        out_shape=[
            jax.ShapeDtypeStruct((batch * n_main * c, D_MODEL), BF16),
            jax.ShapeDtypeStruct((batch, H_A, DK, DV), F32),
        ],
        scratch_shapes=[
            pltpu.VMEM((H_A, DV, DK), F32),
            pltpu.VMEM((HIST_ROWS, D_POOL), F32),
        ],
        compiler_params=_cparams(("parallel", "arbitrary")),
        name="mixer",
    )(*acts, s0, hist, out_norm.reshape(1, D_HGRN), pool_w_bf, pool_scale.reshape(1, D_POOL))


def _outproj_kernel(h_ref, mix_ref, wout_ref, nw_ref, h1_ref, xn_ref):
    h1 = h_ref[...] + jnp.dot(mix_ref[...], wout_ref[...], preferred_element_type=F32)
    h1_ref[...] = h1
    ms = jnp.mean(h1 * h1, axis=-1, keepdims=True)
    xn_ref[...] = h1 * lax.rsqrt(ms + EPS) * nw_ref[...]


def _outproj(h2d, mix, w_out_bf, norm_w):
    n = h2d.shape[0]
    tm = TM_PROJ
    assert n % tm == 0
    row = lambda i: (i, 0)
    const = lambda i: (0, 0)
    out = jax.ShapeDtypeStruct((n, D_MODEL), F32)
    return pl.pallas_call(
        _outproj_kernel,
        grid=(n // tm,),
        in_specs=[
            pl.BlockSpec((tm, D_MODEL), row),
            pl.BlockSpec((tm, D_MODEL), row),
            pl.BlockSpec((D_MODEL, D_MODEL), const),
            pl.BlockSpec((1, D_MODEL), const),
        ],
        out_specs=[pl.BlockSpec((tm, D_MODEL), row)] * 2,
        out_shape=[out, out],
        compiler_params=_cparams(("parallel",)),
        name="outproj",
    )(h2d, mix, w_out_bf, norm_w.reshape(1, D_MODEL))


def _keymat_kernel(sk_ref, wq_ref, m_ref):
    m = lax.dot_general(sk_ref[0], wq_ref[...], (((1,), (1,)), ((), ())),
                        precision=lax.Precision.HIGHEST, preferred_element_type=F32)
    m_ref[...] = m.astype(BF16)


def _keymat(subkeys, w_query):
    nhp = PEER_HEADS * 2
    sk = subkeys.reshape(nhp, N_KEYS, N_KEYS)
    return pl.pallas_call(
        _keymat_kernel,
        grid=(nhp,),
        in_specs=[
            pl.BlockSpec((1, N_KEYS, N_KEYS), lambda i: (i, 0, 0)),
            pl.BlockSpec((D_MODEL, N_KEYS), lambda i: (0, i)),
        ],
        out_specs=pl.BlockSpec((N_KEYS, D_MODEL), lambda i: (i, 0)),
        out_shape=jax.ShapeDtypeStruct((nhp * N_KEYS, D_MODEL), BF16),
        compiler_params=_cparams(("parallel",)),
        name="keymat",
    )(sk, w_query)


_STAIR = [(i, j) for i in range(PEER_TOPK) for j in range(PEER_TOPK) if (i + 1) * (j + 1) <= PEER_TOPK]


def _topk_kernel(xn_ref, m_ref, exp_ref, gate_ref, sc_scr, sv_scr, si_scr):
    tb = xn_ref.shape[0]
    xb = xn_ref[...].astype(BF16)
    sc_scr[...] = lax.dot_general(m_ref[...], xb, (((1,), (1,)), ((), ())), preferred_element_type=F32)
    key_i = lax.broadcasted_iota(jnp.int32, (N_KEYS, tb), 0).astype(F32)

    def head_body(h, carry):
        ss = [sc_scr[pl.ds(pl.multiple_of((2 * h + p) * N_KEYS, N_KEYS), N_KEYS), :] for p in range(2)]
        for i in range(PEER_TOPK):
            for p in range(2):
                s = ss[p]
                mx = jnp.max(s, axis=0, keepdims=True)
                idx = jnp.min(jnp.where(s == mx, key_i, float(N_KEYS)), axis=0, keepdims=True)
                ss[p] = jnp.where(key_i == idx, -jnp.inf, s)
                sv_scr[p, i, pl.ds(h, 1), :] = mx
                si_scr[p, i, pl.ds(h, 1), :] = idx
        return carry

    lax.fori_loop(0, PEER_HEADS, head_body, 0)

    sv0 = [sv_scr[0, i] for i in range(PEER_TOPK)]
    sv1 = [sv_scr[1, i] for i in range(PEER_TOPK)]
    si0 = [si_scr[0, i] for i in range(PEER_TOPK)]
    si1 = [si_scr[1, i] for i in range(PEER_TOPK)]
    cands = [sv0[i] + sv1[j] for (i, j) in _STAIR]
    flats = [float(i * PEER_TOPK + j) for (i, j) in _STAIR]
    tops, experts = [], []
    for r in range(PEER_TOPK):
        mx = functools.reduce(jnp.maximum, cands)
        sel = functools.reduce(jnp.minimum, [jnp.where(cv == mx, fl, 1e9) for cv, fl in zip(cands, flats)])
        cands = [jnp.where(sel == fl, -jnp.inf, cv) for cv, fl in zip(cands, flats)]
        fi = jnp.floor(sel * (1.0 / PEER_TOPK))
        fj = sel - fi * PEER_TOPK
        e0 = functools.reduce(jnp.add, [jnp.where(fi == float(i), si0[i], 0.0) for i in range(PEER_TOPK)])
        e1 = functools.reduce(jnp.add, [jnp.where(fj == float(j), si1[j], 0.0) for j in range(PEER_TOPK)])
        tops.append(mx)
        experts.append((e0 * N_KEYS + e1).astype(jnp.int32))
    ex = [jnp.exp(t - tops[0]) for t in tops]
    den = functools.reduce(jnp.add, ex)
    for r in range(PEER_TOPK):
        exp_ref[0, r] = experts[r]
        gate_ref[0, r] = ex[r] / den


def _topk(xn, keymat):
    n = xn.shape[0]
    tb = TB_TOPK
    assert n % tb == 0
    nb = n // tb
    shp = (nb, PEER_TOPK, PEER_HEADS, tb)
    ospec = pl.BlockSpec((1, PEER_TOPK, PEER_HEADS, tb), lambda i: (i, 0, 0, 0))
    return pl.pallas_call(
        _topk_kernel,
        grid=(nb,),
        in_specs=[
            pl.BlockSpec((tb, D_MODEL), lambda i: (i, 0)),
            pl.BlockSpec(keymat.shape, lambda i: (0, 0)),
        ],
        out_specs=[ospec, ospec],
        out_shape=[jax.ShapeDtypeStruct(shp, jnp.int32), jax.ShapeDtypeStruct(shp, F32)],
        scratch_shapes=[
            pltpu.VMEM((2 * PEER_HEADS * N_KEYS, tb), F32),
            pltpu.VMEM((2, PEER_TOPK, PEER_HEADS, tb), F32),
            pltpu.VMEM((2, PEER_TOPK, PEER_HEADS, tb), F32),
        ],
        compiler_params=_cparams(("parallel",)),
        name="topk",
    )(xn, keymat)


HALF_TILES = D_MODEL // 128


SUB = 8


def _as_tile(row):
    return jnp.concatenate([row[:, c * 128:(c + 1) * 128] for c in range(HALF_TILES)], axis=0)


def _as_row(tile):
    return jnp.concatenate([tile[c:c + 1, :] for c in range(HALF_TILES)], axis=1)


def _peer_kernel(first_ref, ahead_ref, gate_ref, xn_ref, h1_ref, nf_ref, gsum_ref, tab_ref, y_ref,
                 buf, w_scr, sem, *, n_groups, staged):
    tb = xn_ref.shape[0]
    gps = tb // SUB
    step = pl.program_id(0)
    n_tok = n_groups * SUB

    def start_row(row_ref, j, slot, e):
        pltpu.make_async_copy(tab_ref.at[row_ref[j, e]], buf.at[slot, e], sem.at[slot]).start(priority=e % 2)

    def start_token(tok, slot):
        src = jnp.minimum(tok, n_tok - 1) * HITS
        pltpu.make_async_copy(tab_ref.at[pl.ds(src, HITS)], buf.at[slot], sem.at[slot]).start()

    def gather_wait(slot):
        pltpu.make_async_copy(tab_ref.at[pl.ds(0, HITS)], buf.at[slot], sem.at[slot]).wait()

    @pl.when(step == 0)
    def _():
        for t in range(SUB):
            if staged:
                start_token(t, t)
            else:
                for e in range(HITS):
                    start_row(first_ref, t, t, e)

    eye = (lax.broadcasted_iota(jnp.int32, (HITS, HITS), 0)
           == lax.broadcasted_iota(jnp.int32, (HITS, HITS), 1)).astype(F32)
    nf = _as_tile(nf_ref[...])
    gs = gsum_ref[...]

    def u_tile(slot, e):
        return lax.bitcast_convert_type(jnp.left_shift(buf[slot, e], 16), F32)

    def v_tile(slot, e):
        return lax.bitcast_convert_type(jnp.bitwise_and(buf[slot, e], jnp.int32(-65536)), F32)

    def v_row(v_slot, w_half, i, e):
        return w_scr[w_half, i, pl.ds(e, 1), :] * v_tile(v_slot + i, e)

    def finish(accs, h_row):
        y3 = _as_tile(h_row) + ((accs[0] + accs[1]) + (accs[2] + accs[3]))
        ms = jnp.sum(jnp.sum(y3 * y3, axis=1, keepdims=True), axis=0, keepdims=True) * (1.0 / D_MODEL)
        return _as_row(y3 * lax.rsqrt(ms + EPS) * nf)

    def group(g, with_prev):
        base = pl.multiple_of(g * SUB, SUB) if with_prev else 0
        gg = step * gps + g
        cur = lax.rem(gg, 3) * SUB
        nxt = lax.rem(gg + 1, 3) * SUB
        prv = lax.rem(gg + 2, 3) * SUB
        w_cur = lax.rem(gg, 2)
        x8 = xn_ref[pl.ds(base, SUB), :]
        g8 = gate_ref[pl.ds(base, SUB), :]
        if with_prev:
            hp8 = h1_ref[pl.ds(base - SUB, SUB), :]

        for i in range(SUB):
            gather_wait(cur + i)
        parts, rows = [], []
        for i in range(SUB):
            x3 = _as_tile(x8[i:i + 1, :])
            accs = [jnp.zeros((HALF_TILES, 128), F32) for _ in range(4)]
            chunks = []
            if staged:
                start_token(step * tb + base + i + SUB, nxt + i)
            for k in range(HITS // CHUNK_HITS):
                prods = []
                for e in range(k * CHUNK_HITS, (k + 1) * CHUNK_HITS):
                    if not staged:
                        start_row(ahead_ref, base + i, nxt + i, e)
                    prods.append(u_tile(cur + i, e) * x3)
                    if with_prev:
                        accs[e % 4] = accs[e % 4] + v_row(prv, 1 - w_cur, i, e)
                pk = jnp.concatenate(prods, axis=0).astype(BF16)
                chunks.append(jnp.dot(gs, pk, preferred_element_type=F32))
            parts.append(jnp.concatenate(chunks, axis=0))
            if with_prev:
                rows.append(finish(accs, hp8[i:i + 1, :]))
        if with_prev:
            y_ref[pl.ds(base - SUB, SUB), :] = jnp.concatenate(rows, axis=0)

        for i in range(SUB):
            a_col = jnp.sum(parts[i], axis=1, keepdims=True)
            g_col = jnp.sum(eye * g8[i:i + 1, :], axis=1, keepdims=True)
            w_col = g_col * (0.5 * a_col * (1.0 + lax.erf(a_col * (2.0 ** -0.5))))
            w_scr[w_cur, i] = jnp.broadcast_to(w_col, (HITS, 128))

    group(0, False)

    def loop_body(g, carry):
        group(g, True)
        return carry

    lax.fori_loop(1, gps, loop_body, 0)

    gg = step * gps + (gps - 1)
    cur = lax.rem(gg, 3) * SUB
    w_cur = lax.rem(gg, 2)
    h8 = h1_ref[pl.ds(tb - SUB, SUB), :]
    rows = []
    for i in range(SUB):
        accs = [jnp.zeros((HALF_TILES, 128), F32) for _ in range(4)]
        for e in range(HITS):
            accs[e % 4] = accs[e % 4] + v_row(cur, w_cur, i, e)
        rows.append(finish(accs, h8[i:i + 1, :]))
    y_ref[pl.ds(tb - SUB, SUB), :] = jnp.concatenate(rows, axis=0)

    @pl.when(step == pl.num_programs(0) - 1)
    def _():
        for t in range(SUB):
            gather_wait((n_groups % 3) * SUB + t)


def _sc_gather(table, idx_flat):
    info = plsc.get_sparse_core_info()
    n_workers = info.num_cores * info.num_subcores
    rows = idx_flat.shape[0]
    per_w = rows // n_workers
    assert per_w * n_workers == rows and per_w % SC_WINDOW == 0
    mesh = plsc.VectorSubcoreMesh(core_axis_name="c", subcore_axis_name="s")

    def body(tab_hbm, idx_hbm, out_hbm, idx_v, rows_v, sem):
        wid = lax.axis_index("s") * info.num_cores + lax.axis_index("c")
        base = wid * per_w

        @pl.loop(0, per_w // SC_WINDOW)
        def _(w):
            off = base + w * SC_WINDOW
            pltpu.sync_copy(idx_hbm.at[pl.ds(off, SC_WINDOW)], idx_v)
            pltpu.async_copy(tab_hbm.at[idx_v], rows_v, sem).wait()
            pltpu.sync_copy(rows_v, out_hbm.at[pl.ds(off, SC_WINDOW)])

    return pl.kernel(
        body,
        out_type=jax.ShapeDtypeStruct((rows,) + table.shape[1:], table.dtype),
        mesh=mesh,
        scratch_types=[
            pltpu.VMEM((SC_WINDOW,), jnp.int32),
            pltpu.VMEM((SC_WINDOW,) + table.shape[1:], table.dtype),
            pltpu.SemaphoreType.DMA,
        ],
    )(table, idx_flat)


def _peer(idx, gates, xn, h1, norm_final, table, staged=False):
    n = xn.shape[0]
    tb = TB_PEER
    assert n % tb == 0 and tb % SUB == 0 and tb // SUB >= 2
    assert AHEAD == SUB and N_SLOTS == 3 * SUB and HITS % CHUNK_HITS == 0
    r = jnp.arange(CHUNK_HITS * HALF_TILES, dtype=jnp.int32)[None, :] // HALF_TILES
    gsum = (r == jnp.arange(CHUNK_HITS, dtype=jnp.int32)[:, None]).astype(BF16)
    ahead = jnp.concatenate([idx[AHEAD:], jnp.broadcast_to(idx[n - 1:], (AHEAD, HITS))], axis=0)
    first = idx[:SUB]
    row = lambda i: (i, 0)
    const = lambda i: (0, 0)
    return pl.pallas_call(
        functools.partial(_peer_kernel, n_groups=n // SUB, staged=staged),
        grid=(n // tb,),
        in_specs=[
            pl.BlockSpec((SUB, HITS), const, memory_space=pltpu.SMEM),
            pl.BlockSpec((tb, HITS), row, memory_space=pltpu.SMEM),
            pl.BlockSpec((tb, HITS), row),
            pl.BlockSpec((tb, D_MODEL), row),
            pl.BlockSpec((tb, D_MODEL), row),
            pl.BlockSpec((1, D_MODEL), const),
            pl.BlockSpec(gsum.shape, const),
            pl.BlockSpec(memory_space=pl.ANY),
        ],
        out_specs=pl.BlockSpec((tb, D_MODEL), row),
        out_shape=jax.ShapeDtypeStruct((n, D_MODEL), F32),
        scratch_shapes=[
            pltpu.VMEM((N_SLOTS, HITS, HALF_TILES, 128), jnp.int32),
            pltpu.VMEM((2, SUB, HITS, 128), F32),
            pltpu.SemaphoreType.DMA((N_SLOTS,)),
        ],
        compiler_params=_cparams(("arbitrary",)),
        name="peer",
    )(first, ahead, gates, xn, h1, norm_final.reshape(1, D_MODEL), gsum, table)


def _token_major(a):
    nb, r, h, t = a.shape
    return jnp.transpose(a, (0, 3, 2, 1)).reshape(nb * t, h * r)


def _stream(h_pad, h_main, s0, hist, n_hist, pad, skip, c, consts):
    (norm_mix, w_in_bf, lb_logits, out_norm, pool_w_bf, pool_scale, w_out_bf, norm_ffn,
     keymat, table, norm_final) = consts
    batch, t_len, _ = h_pad.shape
    acts = _inproj(h_pad.reshape(batch * t_len, D_MODEL), norm_mix, w_in_bf, lb_logits)
    mix, s_new = _mixer(acts, s0, hist, out_norm, pool_w_bf, pool_scale,
                        batch=batch, t_len=t_len, c=c, pad=pad, n_hist=n_hist, skip=skip)
    zp = acts[5].reshape(batch, t_len, D_POOL)
    new_hist = zp[:, t_len - POOL_HIST:, :]
    hm = h_main.reshape(-1, D_MODEL)
    h1, xn = _outproj(hm, mix, w_out_bf, norm_ffn)
    experts, gates = _topk(xn, keymat)
    idx = _token_major(experts)
    gts = _token_major(gates)
    n = idx.shape[0]
    n_sc = (n * SC_SHARE_PCT // 100) // (2 * TB_PEER) * (2 * TB_PEER) if n >= SC_MIN_TOKENS else 0
    if n_sc == 0:
        y = _peer(idx, gts, xn, h1, norm_final, table)
    else:
        n_tc = n - n_sc
        rows_sc = _sc_gather(table, idx[n_tc:].reshape(-1))
        y_tc = _peer(idx[:n_tc], gts[:n_tc], xn[:n_tc], h1[:n_tc], norm_final, table)
        y_sc = _peer(idx[n_tc:], gts[n_tc:], xn[n_tc:], h1[n_tc:], norm_final, rows_sc, staged=True)
        y = jnp.concatenate([y_tc, y_sc], axis=0)
    return y.reshape(h_main.shape), s_new, new_hist


def kernel(x_prompt, x_sample, state_hgrn, cache_pool, meta_tokens, norm_mix, w_in, hgrn_lb_logits,
           hgrn_out_norm, pool_w, pool_scale, w_out, norm_ffn, peer_w_query, peer_subkeys, peer_u,
           peer_v, norm_final):
    assert state_hgrn.shape[0] == 1, "single trunk layer"
    bp, seq, _ = x_prompt.shape
    bs, dseq, _ = x_sample.shape
    dt = x_prompt.dtype

    keymat = _keymat(peer_subkeys[0], peer_w_query[0])
    u16 = lax.bitcast_convert_type(peer_u[0].astype(BF16), jnp.uint16).astype(jnp.uint32)
    v16 = lax.bitcast_convert_type(peer_v[0].astype(BF16), jnp.uint16).astype(jnp.uint32)
    table = lax.bitcast_convert_type(u16 | (v16 << 16), jnp.int32).reshape(N_EXPERTS, HALF_TILES, 128)
    consts = (norm_mix[0], w_in[0].astype(BF16), hgrn_lb_logits, hgrn_out_norm[0],
              pool_w[0].astype(BF16), pool_scale[0], w_out[0].astype(BF16), norm_ffn[0],
              keymat, table, norm_final)

    pad = CHUNK - N_META
    head = jnp.concatenate([jnp.zeros((bp, pad, D_MODEL), dt),
                            jnp.broadcast_to(meta_tokens[None].astype(dt), (bp, N_META, D_MODEL))], axis=1)
    hp = jnp.concatenate([head, x_prompt], axis=1)
    s_p0 = jnp.zeros((bp, H_A, DK, DV), F32)
    hist_p0 = jnp.zeros((bp, HIST_ROWS, D_POOL), dt)
    y_p, s_p, c_p = _stream(hp, x_prompt, s_p0, hist_p0, 0, pad, 1, CHUNK, consts)

    hist_s0 = jnp.concatenate([jnp.zeros((bs, HIST_ROWS - POOL_HIST, D_POOL), dt), cache_pool[0]], axis=1)
    y_s, s_s, c_s = _stream(x_sample, x_sample, state_hgrn[0], hist_s0, POOL_HIST, 0, 0, dseq, consts)

    return (y_p, y_s, s_p[None].astype(dt), c_p[None], s_s[None].astype(dt), c_s[None])
```

```python
import functools

import jax
import jax.numpy as jnp
from jax import lax
from jax.experimental import pallas as pl
from jax.experimental.pallas import tpu as pltpu
from jax.experimental.pallas import tpu_sc as plsc

F32 = jnp.float32
BF16 = jnp.bfloat16

D_MODEL = 1024
N_META = 16
D_HGRN = 512
D_POOL = 512
H_A = 4
DK = 128
DV = 128
POOL_WINDOWS = (2, 4, 8, 16)
POOL_GW = 128
POOL_HIST = 15
D_IN_PROJ = 4 * D_HGRN + D_POOL
PEER_HEADS = 8
N_KEYS = 128
N_EXPERTS = N_KEYS * N_KEYS
PEER_TOPK = 16
EPS = 1e-6
CHUNK = 64
HIST_ROWS = 16
BASE = 8

VMEM_LIMIT_BYTES = 48 * 1024 * 1024

TM_PROJ = 256
TB_TOPK = 128
TB_PEER = 64
N_SLOTS = 24
AHEAD = 8
CHUNK_HITS = 16
SC_WINDOW = 32
SC_SHARE_PCT = 50
SC_MIN_TOKENS = 4096
HITS = PEER_HEADS * PEER_TOPK
NEG_BIG = -1e30


def _cparams(sem):
    return pltpu.CompilerParams(dimension_semantics=sem, vmem_limit_bytes=VMEM_LIMIT_BYTES)


def _inproj_kernel(h_ref, nw_ref, win_ref, lbl_ref, q_ref, k_ref, lf_ref, v_ref, g_ref, zp_ref):
    x = h_ref[...]
    ms = jnp.mean(x * x, axis=-1, keepdims=True)
    xn = x * lax.rsqrt(ms + EPS) * nw_ref[...]
    proj = jnp.dot(xn.astype(BF16), win_ref[...], preferred_element_type=F32)
    zq = proj[:, 0:D_HGRN]
    zf = proj[:, D_HGRN:2 * D_HGRN]
    zi = proj[:, 2 * D_HGRN:3 * D_HGRN]
    zg = proj[:, 3 * D_HGRN:4 * D_HGRN]
    zp = proj[:, 4 * D_HGRN:]
    lg = lbl_ref[...]
    e = jnp.exp(lg - jnp.max(lg, axis=0, keepdims=True))
    lb = e[0:1, :] / jnp.sum(e, axis=0, keepdims=True)
    q_ref[...] = jax.nn.silu(zq) * (DK ** -0.5)
    lf_ref[...] = jnp.log(lb + (1.0 - lb) * jax.nn.sigmoid(zf))
    k_ref[...] = (1.0 - lb) * jax.nn.sigmoid(-zf)
    v_ref[...] = zi
    g_ref[...] = jax.nn.silu(zg)
    zp_ref[...] = zp


def _inproj(h2d, norm_w, w_in_bf, lb_logits):
    n = h2d.shape[0]
    tm = TM_PROJ
    assert n % tm == 0
    row = lambda i: (i, 0)
    const = lambda i: (0, 0)
    out = jax.ShapeDtypeStruct((n, D_HGRN), F32)
    return pl.pallas_call(
        _inproj_kernel,
        grid=(n // tm,),
        in_specs=[
            pl.BlockSpec((tm, D_MODEL), row),
            pl.BlockSpec((1, D_MODEL), const),
            pl.BlockSpec((D_MODEL, D_IN_PROJ), const),
            pl.BlockSpec(lb_logits.shape, const),
        ],
        out_specs=[pl.BlockSpec((tm, D_HGRN), row)] * 6,
        out_shape=[out] * 6,
        compiler_params=_cparams(("parallel",)),
        name="inproj",
    )(h2d, norm_w.reshape(1, D_MODEL), w_in_bf, lb_logits)


def _levels(c):
    out, m = [], BASE
    while m < c:
        out.append(m)
        m *= 2
    return out


def _blk(x, m):
    return jnp.right_shift(x, m.bit_length() - 1)


def _decay_sum_matrix(c):
    t = lax.broadcasted_iota(jnp.int32, (c, c), 0)
    r = lax.broadcasted_iota(jnp.int32, (c, c), 1)
    mats = [r <= t]
    for m in _levels(c):
        lo = _blk(t, m) * m
        mats.append((r >= lo) & (r <= t))
        mats.append((r > t) & (r <= lo + m - 1))
    return jnp.concatenate([x.astype(F32) for x in mats], axis=0).astype(BF16)


def _mixer_kernel(q_ref, k_ref, lf_ref, v_ref, g_ref, zp_ref, s0_ref, hist_ref, onorm_ref,
                  pw_ref, ps_ref, mix_ref, sout_ref, st_scr, prev_scr, *, c, pad, n_hist):
    ci = pl.program_id(1)
    nc = pl.num_programs(1)

    @pl.when(ci == 0)
    def _():
        for hd in range(H_A):
            st_scr[hd] = s0_ref[0, hd].T
        prev_scr[...] = hist_ref[0]

    row = ci * c + lax.broadcasted_iota(jnp.int32, (c, 1), 0)
    live = row >= pad
    lf_all = jnp.where(live, lf_ref[...], 0.0)
    k_all = jnp.where(live, k_ref[...], 0.0)
    q_all = q_ref[...]
    v_all = v_ref[...]

    wmat = _decay_sum_matrix(c)
    lf_hi = lf_all.astype(BF16)
    r1 = lf_all - lf_hi.astype(F32)
    lf_mid = r1.astype(BF16)
    lf_lo = (r1 - lf_mid.astype(F32)).astype(BF16)
    sums = (jnp.dot(wmat, lf_hi, preferred_element_type=F32)
            + jnp.dot(wmat, lf_mid, preferred_element_type=F32)
            + jnp.dot(wmat, lf_lo, preferred_element_type=F32))

    levels = _levels(c)
    t_i = lax.broadcasted_iota(jnp.int32, (c, c), 0)
    s_i = lax.broadcasted_iota(jnp.int32, (c, c), 1)
    row_i = lax.broadcasted_iota(jnp.int32, (c, 1), 0)
    sub_i = lax.broadcasted_iota(jnp.int32, (c // BASE, BASE, 1), 1)
    ones_b = jnp.ones((DK, DV), BF16)
    onorm = onorm_ref[...]
    g_all = g_ref[...]

    o_heads = []
    for hd in range(H_A):
        sl = slice(hd * DK, (hd + 1) * DK)
        q = q_all[:, sl]
        k = k_all[:, sl]
        v = v_all[:, sl]
        b = sums[0:c, sl]
        st = st_scr[hd]

        o = lax.dot_general((q * jnp.exp(b)).astype(BF16), st.astype(BF16),
                            (((1,), (1,)), ((), ())), preferred_element_type=F32)

        a_mat = jnp.zeros((c, c), F32)
        for li, m in enumerate(levels):
            eq = sums[(1 + 2 * li) * c:(2 + 2 * li) * c, sl]
            ek = sums[(2 + 2 * li) * c:(3 + 2 * li) * c, sl]
            odd = (_blk(row_i, m) & 1) == 1
            qs = q * jnp.exp(jnp.where(odd, eq, NEG_BIG))
            ks = k * jnp.exp(jnp.where(odd, NEG_BIG, ek))
            am = lax.dot_general(qs.astype(BF16), ks.astype(BF16),
                                 (((1,), (1,)), ((), ())), preferred_element_type=F32)
            pair = ((_blk(t_i, 2 * m) == _blk(s_i, 2 * m))
                    & ((_blk(t_i, m) & 1) == 1) & ((_blk(s_i, m) & 1) == 0))
            a_mat = a_mat + jnp.where(pair, am, 0.0)
        o = o + jnp.dot(a_mat.astype(BF16), v.astype(BF16), preferred_element_type=F32)

        q3 = q.reshape(c // BASE, BASE, DK)
        k3 = k.reshape(c // BASE, BASE, DK)
        v3 = v.reshape(c // BASE, BASE, DV)
        b3 = b.reshape(c // BASE, BASE, DK)
        prods = []
        for d in range(BASE):
            if d == 0:
                prods.append(q3 * k3)
            else:
                kd = pltpu.roll(k3, d, 1)
                bd = pltpu.roll(b3, d, 1)
                dec = jnp.exp(jnp.where(sub_i >= d, b3 - bd, NEG_BIG))
                prods.append(q3 * kd * dec)
        pst = jnp.concatenate(prods, axis=0).reshape(BASE * c, DK)
        asum = jnp.dot(pst.astype(BF16), ones_b, preferred_element_type=F32)
        for d in range(BASE):
            vd = v if d == 0 else pltpu.roll(v3, d, 1).reshape(c, DV)
            o = o + asum[d * c:(d + 1) * c, :] * vd

        bl = b[c - 1:c, :]
        kdec = k * jnp.exp(bl - b)
        upd = lax.dot_general(v.astype(BF16), kdec.astype(BF16),
                              (((0,), (0,)), ((), ())), preferred_element_type=F32)
        st_new = st * jnp.exp(bl) + upd
        st_scr[hd] = st_new

        @pl.when(ci == nc - 1)
        def _():
            sout_ref[0, hd] = st_new.T

        o = o * lax.rsqrt(jnp.mean(o * o, axis=-1, keepdims=True) + EPS)
        o_heads.append(o * onorm[:, sl] * g_all[:, sl])

    zp = zp_ref[...]
    ext = jnp.concatenate([prev_scr[...], zp], axis=0)
    tok = (row - pad).astype(F32)
    avail = jnp.maximum(n_hist + tok + 1.0, 1.0)
    pool_heads = []
    for gi, w in enumerate(POOL_WINDOWS):
        sl = slice(gi * POOL_GW, (gi + 1) * POOL_GW)
        s = ext[:, sl]
        sh = 1
        while sh < w:
            s = s + pltpu.roll(s, sh, 0)
            sh *= 2
        win = s[HIST_ROWS:, :]
        cnt = jnp.minimum(float(w), avail)
        dlt = win / cnt - zp[:, sl]
        po = jnp.dot(dlt.astype(BF16), pw_ref[gi], preferred_element_type=F32)
        pool_heads.append(po * ps_ref[:, sl])
    prev_scr[...] = zp[c - HIST_ROWS:, :]

    mix_ref[...] = jnp.concatenate(o_heads + pool_heads, axis=1).astype(BF16)


def _mixer(acts, s0, hist, out_norm, pool_w_bf, pool_scale, *, batch, t_len, c, pad, n_hist, skip):
    nc = t_len // c
    assert nc * c == t_len and c % BASE == 0 and c >= HIST_ROWS
    blk = lambda b, ci: (b * nc + ci, 0)
    per_b4 = lambda b, ci: (b, 0, 0, 0)
    per_b3 = lambda b, ci: (b, 0, 0)
    c2 = lambda b, ci: (0, 0)
    c3 = lambda b, ci: (0, 0, 0)
    n_main = nc - skip
    mix_map = lambda b, ci: (b * n_main + jnp.maximum(ci - skip, 0), 0)
    kern = functools.partial(_mixer_kernel, c=c, pad=pad, n_hist=float(n_hist))
    return pl.pallas_call(
        kern,
        grid=(batch, nc),
        in_specs=[pl.BlockSpec((c, D_HGRN), blk)] * 6 + [
            pl.BlockSpec((1, H_A, DK, DV), per_b4),
            pl.BlockSpec((1, HIST_ROWS, D_POOL), per_b3),
            pl.BlockSpec((1, D_HGRN), c2),
            pl.BlockSpec((len(POOL_WINDOWS), POOL_GW, POOL_GW), c3),
            pl.BlockSpec((1, D_POOL), c2),
        ],
        out_specs=[
            pl.BlockSpec((c, D_MODEL), mix_map),
            pl.BlockSpec((1, H_A, DK, DV), per_b4),
        ],
        out_shape=[
            jax.ShapeDtypeStruct((batch * n_main * c, D_MODEL), BF16),
            jax.ShapeDtypeStruct((batch, H_A, DK, DV), F32),
        ],
        scratch_shapes=[
            pltpu.VMEM((H_A, DV, DK), F32),
            pltpu.VMEM((HIST_ROWS, D_POOL), F32),
        ],
        compiler_params=_cparams(("parallel", "arbitrary")),
        name="mixer",
    )(*acts, s0, hist, out_norm.reshape(1, D_HGRN), pool_w_bf, pool_scale.reshape(1, D_POOL))


def _outproj_kernel(h_ref, mix_ref, wout_ref, nw_ref, h1_ref, xn_ref):
    h1 = h_ref[...] + jnp.dot(mix_ref[...], wout_ref[...], preferred_element_type=F32)
    h1_ref[...] = h1
    ms = jnp.mean(h1 * h1, axis=-1, keepdims=True)
    xn_ref[...] = h1 * lax.rsqrt(ms + EPS) * nw_ref[...]


def _outproj(h2d, mix, w_out_bf, norm_w):
    n = h2d.shape[0]
    tm = TM_PROJ
    assert n % tm == 0
    row = lambda i: (i, 0)
    const = lambda i: (0, 0)
    out = jax.ShapeDtypeStruct((n, D_MODEL), F32)
    return pl.pallas_call(
        _outproj_kernel,
        grid=(n // tm,),
        in_specs=[
            pl.BlockSpec((tm, D_MODEL), row),
            pl.BlockSpec((tm, D_MODEL), row),
            pl.BlockSpec((D_MODEL, D_MODEL), const),
            pl.BlockSpec((1, D_MODEL), const),
        ],
        out_specs=[pl.BlockSpec((tm, D_MODEL), row)] * 2,
        out_shape=[out, out],
        compiler_params=_cparams(("parallel",)),
        name="outproj",
    )(h2d, mix, w_out_bf, norm_w.reshape(1, D_MODEL))


def _keymat_kernel(sk_ref, wq_ref, m_ref):
    m = lax.dot_general(sk_ref[0], wq_ref[...], (((1,), (1,)), ((), ())),
                        precision=lax.Precision.HIGHEST, preferred_element_type=F32)
    m_ref[...] = m.astype(BF16)


def _keymat(subkeys, w_query):
    nhp = PEER_HEADS * 2
    sk = subkeys.reshape(nhp, N_KEYS, N_KEYS)
    return pl.pallas_call(
        _keymat_kernel,
        grid=(nhp,),
        in_specs=[
            pl.BlockSpec((1, N_KEYS, N_KEYS), lambda i: (i, 0, 0)),
            pl.BlockSpec((D_MODEL, N_KEYS), lambda i: (0, i)),
        ],
        out_specs=pl.BlockSpec((N_KEYS, D_MODEL), lambda i: (i, 0)),
        out_shape=jax.ShapeDtypeStruct((nhp * N_KEYS, D_MODEL), BF16),
        compiler_params=_cparams(("parallel",)),
        name="keymat",
    )(sk, w_query)


_STAIR = [(i, j) for i in range(PEER_TOPK) for j in range(PEER_TOPK) if (i + 1) * (j + 1) <= PEER_TOPK]


def _topk_kernel(xn_ref, m_ref, exp_ref, gate_ref, sc_scr, sv_scr, si_scr):
    tb = xn_ref.shape[0]
    xb = xn_ref[...].astype(BF16)
    sc_scr[...] = lax.dot_general(m_ref[...], xb, (((1,), (1,)), ((), ())), preferred_element_type=F32)
    key_i = lax.broadcasted_iota(jnp.int32, (N_KEYS, tb), 0).astype(F32)

    def head_body(h, carry):
        ss = [sc_scr[pl.ds(pl.multiple_of((2 * h + p) * N_KEYS, N_KEYS), N_KEYS), :] for p in range(2)]
        for i in range(PEER_TOPK):
            for p in range(2):
                s = ss[p]
                mx = jnp.max(s, axis=0, keepdims=True)
                idx = jnp.min(jnp.where(s == mx, key_i, float(N_KEYS)), axis=0, keepdims=True)
                ss[p] = jnp.where(key_i == idx, -jnp.inf, s)
                sv_scr[p, i, pl.ds(h, 1), :] = mx
                si_scr[p, i, pl.ds(h, 1), :] = idx
        return carry

    lax.fori_loop(0, PEER_HEADS, head_body, 0)

    sv0 = [sv_scr[0, i] for i in range(PEER_TOPK)]
    sv1 = [sv_scr[1, i] for i in range(PEER_TOPK)]
    si0 = [si_scr[0, i] for i in range(PEER_TOPK)]
    si1 = [si_scr[1, i] for i in range(PEER_TOPK)]
    cands = [sv0[i] + sv1[j] for (i, j) in _STAIR]
    flats = [float(i * PEER_TOPK + j) for (i, j) in _STAIR]
    tops, experts = [], []
    for r in range(PEER_TOPK):
        mx = functools.reduce(jnp.maximum, cands)
        sel = functools.reduce(jnp.minimum, [jnp.where(cv == mx, fl, 1e9) for cv, fl in zip(cands, flats)])
        cands = [jnp.where(sel == fl, -jnp.inf, cv) for cv, fl in zip(cands, flats)]
        fi = jnp.floor(sel * (1.0 / PEER_TOPK))
        fj = sel - fi * PEER_TOPK
        e0 = functools.reduce(jnp.add, [jnp.where(fi == float(i), si0[i], 0.0) for i in range(PEER_TOPK)])
        e1 = functools.reduce(jnp.add, [jnp.where(fj == float(j), si1[j], 0.0) for j in range(PEER_TOPK)])
        tops.append(mx)
        experts.append((e0 * N_KEYS + e1).astype(jnp.int32))
    ex = [jnp.exp(t - tops[0]) for t in tops]
    den = functools.reduce(jnp.add, ex)
    for r in range(PEER_TOPK):
        exp_ref[0, r] = experts[r]
        gate_ref[0, r] = ex[r] / den


def _topk(xn, keymat):
    n = xn.shape[0]
    tb = TB_TOPK
    assert n % tb == 0
    nb = n // tb
    shp = (nb, PEER_TOPK, PEER_HEADS, tb)
    ospec = pl.BlockSpec((1, PEER_TOPK, PEER_HEADS, tb), lambda i: (i, 0, 0, 0))
    return pl.pallas_call(
        _topk_kernel,
        grid=(nb,),
        in_specs=[
            pl.BlockSpec((tb, D_MODEL), lambda i: (i, 0)),
            pl.BlockSpec(keymat.shape, lambda i: (0, 0)),
        ],
        out_specs=[ospec, ospec],
        out_shape=[jax.ShapeDtypeStruct(shp, jnp.int32), jax.ShapeDtypeStruct(shp, F32)],
        scratch_shapes=[
            pltpu.VMEM((2 * PEER_HEADS * N_KEYS, tb), F32),
            pltpu.VMEM((2, PEER_TOPK, PEER_HEADS, tb), F32),
            pltpu.VMEM((2, PEER_TOPK, PEER_HEADS, tb), F32),
        ],
        compiler_params=_cparams(("parallel",)),
        name="topk",
    )(xn, keymat)


HALF_TILES = D_MODEL // 128


SUB = 8


def _as_tile(row):
    return jnp.concatenate([row[:, c * 128:(c + 1) * 128] for c in range(HALF_TILES)], axis=0)


def _as_row(tile):
    return jnp.concatenate([tile[c:c + 1, :] for c in range(HALF_TILES)], axis=1)


def _peer_kernel(first_ref, ahead_ref, gate_ref, xn_ref, h1_ref, nf_ref, gsum_ref, tab_ref, y_ref,
                 buf, w_scr, sem, *, n_groups, staged):
    tb = xn_ref.shape[0]
    gps = tb // SUB
    step = pl.program_id(0)
    n_tok = n_groups * SUB

    def start_row(row_ref, j, slot, e):
        pltpu.make_async_copy(tab_ref.at[row_ref[j, e]], buf.at[slot, e], sem.at[slot]).start(priority=e % 2)

    def start_token(tok, slot, queue):
        src = jnp.minimum(tok, n_tok - 1) * HITS
        pltpu.make_async_copy(tab_ref.at[pl.ds(src, HITS)], buf.at[slot], sem.at[slot]).start(priority=queue)

    def gather_wait(slot):
        pltpu.make_async_copy(tab_ref.at[pl.ds(0, HITS)], buf.at[slot], sem.at[slot]).wait()

    @pl.when(step == 0)
    def _():
        for t in range(SUB):
            if staged:
                start_token(t, t, t % 2)
            else:
                for e in range(HITS):
                    start_row(first_ref, t, t, e)

    eye = (lax.broadcasted_iota(jnp.int32, (HITS, HITS), 0)
           == lax.broadcasted_iota(jnp.int32, (HITS, HITS), 1)).astype(F32)
    nf = _as_tile(nf_ref[...])
    gs = gsum_ref[...]

    def u_tile(slot, e):
        return lax.bitcast_convert_type(jnp.left_shift(buf[slot, e], 16), F32)

    def v_tile(slot, e):
        return lax.bitcast_convert_type(jnp.bitwise_and(buf[slot, e], jnp.int32(-65536)), F32)

    def v_row(v_slot, w_half, i, e):
        return w_scr[w_half, i, pl.ds(e, 1), :] * v_tile(v_slot + i, e)

    def finish(accs, h_row):
        y3 = _as_tile(h_row) + ((accs[0] + accs[1]) + (accs[2] + accs[3]))
        ms = jnp.sum(jnp.sum(y3 * y3, axis=1, keepdims=True), axis=0, keepdims=True) * (1.0 / D_MODEL)
        return _as_row(y3 * lax.rsqrt(ms + EPS) * nf)

    def group(g, with_prev):
        base = pl.multiple_of(g * SUB, SUB) if with_prev else 0
        gg = step * gps + g
        cur = lax.rem(gg, 3) * SUB
        nxt = lax.rem(gg + 1, 3) * SUB
        prv = lax.rem(gg + 2, 3) * SUB
        w_cur = lax.rem(gg, 2)
        x8 = xn_ref[pl.ds(base, SUB), :]
        g8 = gate_ref[pl.ds(base, SUB), :]
        if with_prev:
            hp8 = h1_ref[pl.ds(base - SUB, SUB), :]

        for i in range(SUB):
            gather_wait(cur + i)
        parts, rows = [], []
        for i in range(SUB):
            x3 = _as_tile(x8[i:i + 1, :])
            accs = [jnp.zeros((HALF_TILES, 128), F32) for _ in range(4)]
            chunks = []
            if staged:
                start_token(step * tb + base + i + SUB, nxt + i, i % 2)
            for k in range(HITS // CHUNK_HITS):
                prods = []
                for e in range(k * CHUNK_HITS, (k + 1) * CHUNK_HITS):
                    if not staged:
                        start_row(ahead_ref, base + i, nxt + i, e)
                    prods.append(u_tile(cur + i, e) * x3)
                    if with_prev:
                        accs[e % 4] = accs[e % 4] + v_row(prv, 1 - w_cur, i, e)
                pk = jnp.concatenate(prods, axis=0).astype(BF16)
                chunks.append(jnp.dot(gs, pk, preferred_element_type=F32))
            parts.append(jnp.concatenate(chunks, axis=0))
            if with_prev:
                rows.append(finish(accs, hp8[i:i + 1, :]))
        if with_prev:
            y_ref[pl.ds(base - SUB, SUB), :] = jnp.concatenate(rows, axis=0)

        for i in range(SUB):
            a_col = jnp.sum(parts[i], axis=1, keepdims=True)
            g_col = jnp.sum(eye * g8[i:i + 1, :], axis=1, keepdims=True)
            w_col = g_col * (0.5 * a_col * (1.0 + lax.erf(a_col * (2.0 ** -0.5))))
            w_scr[w_cur, i] = jnp.broadcast_to(w_col, (HITS, 128))

    group(0, False)

    def loop_body(g, carry):
        group(g, True)
        return carry

    lax.fori_loop(1, gps, loop_body, 0)

    gg = step * gps + (gps - 1)
    cur = lax.rem(gg, 3) * SUB
    w_cur = lax.rem(gg, 2)
    h8 = h1_ref[pl.ds(tb - SUB, SUB), :]
    rows = []
    for i in range(SUB):
        accs = [jnp.zeros((HALF_TILES, 128), F32) for _ in range(4)]
        for e in range(HITS):
            accs[e % 4] = accs[e % 4] + v_row(cur, w_cur, i, e)
        rows.append(finish(accs, h8[i:i + 1, :]))
    y_ref[pl.ds(tb - SUB, SUB), :] = jnp.concatenate(rows, axis=0)

    @pl.when(step == pl.num_programs(0) - 1)
    def _():
        for t in range(SUB):
            gather_wait((n_groups % 3) * SUB + t)


def _sc_gather(table, idx_flat):
    info = plsc.get_sparse_core_info()
    n_workers = info.num_cores * info.num_subcores
    rows = idx_flat.shape[0]
    per_w = rows // n_workers
    assert per_w * n_workers == rows and per_w % SC_WINDOW == 0
    mesh = plsc.VectorSubcoreMesh(core_axis_name="c", subcore_axis_name="s")

    n_win = per_w // SC_WINDOW
    assert n_win % 2 == 0

    def body(tab_hbm, idx_hbm, out_hbm, idx_v, rows_v, sem):
        wid = lax.axis_index("s") * info.num_cores + lax.axis_index("c")
        base = wid * per_w

        def load_idx(win, b):
            pltpu.sync_copy(idx_hbm.at[pl.ds(base + win * SC_WINDOW, SC_WINDOW)], idx_v.at[b])

        def gather(b):
            return pltpu.make_async_copy(tab_hbm.at[idx_v.at[b]], rows_v.at[b], sem.at[b])

        load_idx(0, 0)
        gather(0).start()

        @pl.loop(0, n_win, step=2)
        def _(w):
            for b in range(2):
                cur = w + b
                load_idx(jnp.minimum(cur + 1, n_win - 1), 1 - b)
                gather(1 - b).start()
                gather(b).wait()
                pltpu.sync_copy(rows_v.at[b], out_hbm.at[pl.ds(base + cur * SC_WINDOW, SC_WINDOW)])

        gather(0).wait()

    return pl.kernel(
        body,
        out_type=jax.ShapeDtypeStruct((rows,) + table.shape[1:], table.dtype),
        mesh=mesh,
        scratch_types=[
            pltpu.VMEM((2, SC_WINDOW), jnp.int32),
            pltpu.VMEM((2, SC_WINDOW) + table.shape[1:], table.dtype),
            pltpu.SemaphoreType.DMA((2,)),
        ],
    )(table, idx_flat)


def _peer(idx, gates, xn, h1, norm_final, table, staged=False):
    n = xn.shape[0]
    tb = TB_PEER
    assert n % tb == 0 and tb % SUB == 0 and tb // SUB >= 2
    assert AHEAD == SUB and N_SLOTS == 3 * SUB and HITS % CHUNK_HITS == 0
    r = jnp.arange(CHUNK_HITS * HALF_TILES, dtype=jnp.int32)[None, :] // HALF_TILES
    gsum = (r == jnp.arange(CHUNK_HITS, dtype=jnp.int32)[:, None]).astype(BF16)
    ahead = jnp.concatenate([idx[AHEAD:], jnp.broadcast_to(idx[n - 1:], (AHEAD, HITS))], axis=0)
    first = idx[:SUB]
    row = lambda i: (i, 0)
    const = lambda i: (0, 0)
    return pl.pallas_call(
        functools.partial(_peer_kernel, n_groups=n // SUB, staged=staged),
        grid=(n // tb,),
        in_specs=[
            pl.BlockSpec((SUB, HITS), const, memory_space=pltpu.SMEM),
            pl.BlockSpec((tb, HITS), row, memory_space=pltpu.SMEM),
            pl.BlockSpec((tb, HITS), row),
            pl.BlockSpec((tb, D_MODEL), row),
            pl.BlockSpec((tb, D_MODEL), row),
            pl.BlockSpec((1, D_MODEL), const),
            pl.BlockSpec(gsum.shape, const),
            pl.BlockSpec(memory_space=pl.ANY),
        ],
        out_specs=pl.BlockSpec((tb, D_MODEL), row),
        out_shape=jax.ShapeDtypeStruct((n, D_MODEL), F32),
        scratch_shapes=[
            pltpu.VMEM((N_SLOTS, HITS, HALF_TILES, 128), jnp.int32),
            pltpu.VMEM((2, SUB, HITS, 128), F32),
            pltpu.SemaphoreType.DMA((N_SLOTS,)),
        ],
        compiler_params=_cparams(("arbitrary",)),
        name="peer",
    )(first, ahead, gates, xn, h1, norm_final.reshape(1, D_MODEL), gsum, table)


def _token_major(a):
    nb, r, h, t = a.shape
    return jnp.transpose(a, (0, 3, 2, 1)).reshape(nb * t, h * r)


def _stream(h_pad, h_main, s0, hist, n_hist, pad, skip, c, consts):
    (norm_mix, w_in_bf, lb_logits, out_norm, pool_w_bf, pool_scale, w_out_bf, norm_ffn,
     keymat, table, norm_final) = consts
    batch, t_len, _ = h_pad.shape
    acts = _inproj(h_pad.reshape(batch * t_len, D_MODEL), norm_mix, w_in_bf, lb_logits)
    mix, s_new = _mixer(acts, s0, hist, out_norm, pool_w_bf, pool_scale,
                        batch=batch, t_len=t_len, c=c, pad=pad, n_hist=n_hist, skip=skip)
    zp = acts[5].reshape(batch, t_len, D_POOL)
    new_hist = zp[:, t_len - POOL_HIST:, :]
    hm = h_main.reshape(-1, D_MODEL)
    h1, xn = _outproj(hm, mix, w_out_bf, norm_ffn)
    experts, gates = _topk(xn, keymat)
    idx = _token_major(experts)
    gts = _token_major(gates)
    n = idx.shape[0]
    n_sc = (n * SC_SHARE_PCT // 100) // (2 * TB_PEER) * (2 * TB_PEER) if n >= SC_MIN_TOKENS else 0
    if n_sc == 0:
        y = _peer(idx, gts, xn, h1, norm_final, table)
    else:
        n_tc = n - n_sc
        rows_sc = _sc_gather(table, idx[n_tc:].reshape(-1))
        y_tc = _peer(idx[:n_tc], gts[:n_tc], xn[:n_tc], h1[:n_tc], norm_final, table)
        y_sc = _peer(idx[n_tc:], gts[n_tc:], xn[n_tc:], h1[n_tc:], norm_final, rows_sc, staged=True)
        y = jnp.concatenate([y_tc, y_sc], axis=0)
    return y.reshape(h_main.shape), s_new, new_hist


def kernel(x_prompt, x_sample, state_hgrn, cache_pool, meta_tokens, norm_mix, w_in, hgrn_lb_logits,
           hgrn_out_norm, pool_w, pool_scale, w_out, norm_ffn, peer_w_query, peer_subkeys, peer_u,
           peer_v, norm_final):
    assert state_hgrn.shape[0] == 1, "single trunk layer"
    bp, seq, _ = x_prompt.shape
    bs, dseq, _ = x_sample.shape
    dt = x_prompt.dtype

    keymat = _keymat(peer_subkeys[0], peer_w_query[0])
    u16 = lax.bitcast_convert_type(peer_u[0].astype(BF16), jnp.uint16).astype(jnp.uint32)
    v16 = lax.bitcast_convert_type(peer_v[0].astype(BF16), jnp.uint16).astype(jnp.uint32)
    table = lax.bitcast_convert_type(u16 | (v16 << 16), jnp.int32).reshape(N_EXPERTS, HALF_TILES, 128)
    consts = (norm_mix[0], w_in[0].astype(BF16), hgrn_lb_logits, hgrn_out_norm[0],
              pool_w[0].astype(BF16), pool_scale[0], w_out[0].astype(BF16), norm_ffn[0],
              keymat, table, norm_final)

    pad = CHUNK - N_META
    head = jnp.concatenate([jnp.zeros((bp, pad, D_MODEL), dt),
                            jnp.broadcast_to(meta_tokens[None].astype(dt), (bp, N_META, D_MODEL))], axis=1)
    hp = jnp.concatenate([head, x_prompt], axis=1)
    s_p0 = jnp.zeros((bp, H_A, DK, DV), F32)
    hist_p0 = jnp.zeros((bp, HIST_ROWS, D_POOL), dt)
    y_p, s_p, c_p = _stream(hp, x_prompt, s_p0, hist_p0, 0, pad, 1, CHUNK, consts)

    hist_s0 = jnp.concatenate([jnp.zeros((bs, HIST_ROWS - POOL_HIST, D_POOL), dt), cache_pool[0]], axis=1)
    y_s, s_s, c_s = _stream(x_sample, x_sample, state_hgrn[0], hist_s0, POOL_HIST, 0, 0, dseq, consts)

    return (y_p, y_s, s_p[None].astype(dt), c_p[None], s_s[None].astype(dt), c_s[None])
```

```python
import functools

import jax
import jax.numpy as jnp
from jax import lax
from jax.experimental import pallas as pl
from jax.experimental.pallas import tpu as pltpu

F32 = jnp.float32
BF16 = jnp.bfloat16

D_MODEL = 1024
N_META = 16
D_HGRN = 512
D_POOL = 512
H_A = 4
DK = 128
DV = 128
POOL_WINDOWS = (2, 4, 8, 16)
POOL_GW = 128
POOL_HIST = 15
D_IN_PROJ = 4 * D_HGRN + D_POOL
PEER_HEADS = 8
N_KEYS = 128
N_EXPERTS = N_KEYS * N_KEYS
PEER_TOPK = 16
EPS = 1e-6
CHUNK = 64
HIST_ROWS = 16
BASE = 8

VMEM_LIMIT_BYTES = 48 * 1024 * 1024

TM_PROJ = 256
TB_TOPK = 128
TB_PEER = 64
N_SLOTS = 24
AHEAD = 8
CHUNK_HITS = 16
MIX_BATCH = 2
HITS = PEER_HEADS * PEER_TOPK
NEG_BIG = -1e30


def _cparams(sem):
    return pltpu.CompilerParams(dimension_semantics=sem, vmem_limit_bytes=VMEM_LIMIT_BYTES)


def _inproj_kernel(h_ref, nw_ref, win_ref, lbl_ref, q_ref, k_ref, lf_ref, v_ref, g_ref, zp_ref):
    x = h_ref[...]
    ms = jnp.mean(x * x, axis=-1, keepdims=True)
    xn = x * lax.rsqrt(ms + EPS) * nw_ref[...]
    proj = jnp.dot(xn.astype(BF16), win_ref[...], preferred_element_type=F32)
    zq = proj[:, 0:D_HGRN]
    zf = proj[:, D_HGRN:2 * D_HGRN]
    zi = proj[:, 2 * D_HGRN:3 * D_HGRN]
    zg = proj[:, 3 * D_HGRN:4 * D_HGRN]
    zp = proj[:, 4 * D_HGRN:]
    lg = lbl_ref[...]
    e = jnp.exp(lg - jnp.max(lg, axis=0, keepdims=True))
    lb = e[0:1, :] / jnp.sum(e, axis=0, keepdims=True)
    q_ref[...] = jax.nn.silu(zq) * (DK ** -0.5)
    lf_ref[...] = jnp.log(lb + (1.0 - lb) * jax.nn.sigmoid(zf))
    k_ref[...] = (1.0 - lb) * jax.nn.sigmoid(-zf)
    v_ref[...] = zi
    g_ref[...] = jax.nn.silu(zg)
    zp_ref[...] = zp


def _inproj(h2d, norm_w, w_in_bf, lb_logits):
    n = h2d.shape[0]
    tm = TM_PROJ
    assert n % tm == 0
    row = lambda i: (i, 0)
    const = lambda i: (0, 0)
    out = jax.ShapeDtypeStruct((n, D_HGRN), F32)
    return pl.pallas_call(
        _inproj_kernel,
        grid=(n // tm,),
        in_specs=[
            pl.BlockSpec((tm, D_MODEL), row),
            pl.BlockSpec((1, D_MODEL), const),
            pl.BlockSpec((D_MODEL, D_IN_PROJ), const),
            pl.BlockSpec(lb_logits.shape, const),
        ],
        out_specs=[pl.BlockSpec((tm, D_HGRN), row)] * 6,
        out_shape=[out] * 6,
        compiler_params=_cparams(("parallel",)),
        name="inproj",
    )(h2d, norm_w.reshape(1, D_MODEL), w_in_bf, lb_logits)


def _levels(c):
    out, m = [], BASE
    while m < c:
        out.append(m)
        m *= 2
    return out


def _blk(x, m):
    return jnp.right_shift(x, m.bit_length() - 1)


def _decay_sum_matrix(c):
    t = lax.broadcasted_iota(jnp.int32, (c, c), 0)
    r = lax.broadcasted_iota(jnp.int32, (c, c), 1)
    mats = [r <= t]
    for m in _levels(c):
        lo = _blk(t, m) * m
        mats.append((r >= lo) & (r <= t))
        mats.append((r > t) & (r <= lo + m - 1))
    return jnp.concatenate([x.astype(F32) for x in mats], axis=0).astype(BF16)


def _mixer_kernel(q_ref, k_ref, lf_ref, v_ref, g_ref, zp_ref, s0_ref, hist_ref, onorm_ref,
                  pw_ref, ps_ref, mix_ref, sout_ref, st_scr, prev_scr, *, c, pad, n_hist):
    ci = pl.program_id(1)
    nc = pl.num_programs(1)

    @pl.when(ci == 0)
    def _():
        for hd in range(H_A):
            st_scr[hd] = s0_ref[0, hd].T
        prev_scr[...] = hist_ref[0]

    row = ci * c + lax.broadcasted_iota(jnp.int32, (c, 1), 0)
    live = row >= pad
    lf_all = jnp.where(live, lf_ref[...], 0.0)
    k_all = jnp.where(live, k_ref[...], 0.0)
    q_all = q_ref[...]
    v_all = v_ref[...]

    wmat = _decay_sum_matrix(c)
    lf_hi = lf_all.astype(BF16)
    r1 = lf_all - lf_hi.astype(F32)
    lf_mid = r1.astype(BF16)
    lf_lo = (r1 - lf_mid.astype(F32)).astype(BF16)
    sums = (jnp.dot(wmat, lf_hi, preferred_element_type=F32)
            + jnp.dot(wmat, lf_mid, preferred_element_type=F32)
            + jnp.dot(wmat, lf_lo, preferred_element_type=F32))

    levels = _levels(c)
    t_i = lax.broadcasted_iota(jnp.int32, (c, c), 0)
    s_i = lax.broadcasted_iota(jnp.int32, (c, c), 1)
    row_i = lax.broadcasted_iota(jnp.int32, (c, 1), 0)
    sub_i = lax.broadcasted_iota(jnp.int32, (c // BASE, BASE, 1), 1)
    ones_b = jnp.ones((DK, DV), BF16)
    onorm = onorm_ref[...]
    g_all = g_ref[...]

    o_heads = []
    for hd in range(H_A):
        sl = slice(hd * DK, (hd + 1) * DK)
        q = q_all[:, sl]
        k = k_all[:, sl]
        v = v_all[:, sl]
        b = sums[0:c, sl]
        st = st_scr[hd]

        o = lax.dot_general((q * jnp.exp(b)).astype(BF16), st.astype(BF16),
                            (((1,), (1,)), ((), ())), preferred_element_type=F32)

        a_mat = jnp.zeros((c, c), F32)
        for li, m in enumerate(levels):
            eq = sums[(1 + 2 * li) * c:(2 + 2 * li) * c, sl]
            ek = sums[(2 + 2 * li) * c:(3 + 2 * li) * c, sl]
            odd = (_blk(row_i, m) & 1) == 1
            qs = q * jnp.exp(jnp.where(odd, eq, NEG_BIG))
            ks = k * jnp.exp(jnp.where(odd, NEG_BIG, ek))
            am = lax.dot_general(qs.astype(BF16), ks.astype(BF16),
                                 (((1,), (1,)), ((), ())), preferred_element_type=F32)
            pair = ((_blk(t_i, 2 * m) == _blk(s_i, 2 * m))
                    & ((_blk(t_i, m) & 1) == 1) & ((_blk(s_i, m) & 1) == 0))
            a_mat = a_mat + jnp.where(pair, am, 0.0)
        o = o + jnp.dot(a_mat.astype(BF16), v.astype(BF16), preferred_element_type=F32)

        q3 = q.reshape(c // BASE, BASE, DK)
        k3 = k.reshape(c // BASE, BASE, DK)
        v3 = v.reshape(c // BASE, BASE, DV)
        b3 = b.reshape(c // BASE, BASE, DK)
        prods = []
        for d in range(BASE):
            if d == 0:
                prods.append(q3 * k3)
            else:
                kd = pltpu.roll(k3, d, 1)
                bd = pltpu.roll(b3, d, 1)
                dec = jnp.exp(jnp.where(sub_i >= d, b3 - bd, NEG_BIG))
                prods.append(q3 * kd * dec)
        pst = jnp.concatenate(prods, axis=0).reshape(BASE * c, DK)
        asum = jnp.dot(pst.astype(BF16), ones_b, preferred_element_type=F32)
        for d in range(BASE):
            vd = v if d == 0 else pltpu.roll(v3, d, 1).reshape(c, DV)
            o = o + asum[d * c:(d + 1) * c, :] * vd

        bl = b[c - 1:c, :]
        kdec = k * jnp.exp(bl - b)
        upd = lax.dot_general(v.astype(BF16), kdec.astype(BF16),
                              (((0,), (0,)), ((), ())), preferred_element_type=F32)
        st_scr[hd] = st * jnp.exp(bl) + upd

        o = o * lax.rsqrt(jnp.mean(o * o, axis=-1, keepdims=True) + EPS)
        o_heads.append(o * onorm[:, sl] * g_all[:, sl])

    zp = zp_ref[...]
    ext = jnp.concatenate([prev_scr[...], zp], axis=0)
    tok = (row - pad).astype(F32)
    avail = jnp.maximum(n_hist + tok + 1.0, 1.0)
    pool_heads = []
    for gi, w in enumerate(POOL_WINDOWS):
        sl = slice(gi * POOL_GW, (gi + 1) * POOL_GW)
        s = ext[:, sl]
        sh = 1
        while sh < w:
            s = s + pltpu.roll(s, sh, 0)
            sh *= 2
        win = s[HIST_ROWS:, :]
        cnt = jnp.minimum(float(w), avail)
        dlt = win / cnt - zp[:, sl]
        po = jnp.dot(dlt.astype(BF16), pw_ref[gi], preferred_element_type=F32)
        pool_heads.append(po * ps_ref[:, sl])
    prev_scr[...] = zp[c - HIST_ROWS:, :]

    mix_ref[...] = jnp.concatenate(o_heads + pool_heads, axis=1).astype(BF16)

    @pl.when(ci == nc - 1)
    def _():
        for hd in range(H_A):
            sout_ref[0, hd] = st_scr[hd].T


def _mixer_group_kernel(q_ref, k_ref, lf_ref, v_ref, g_ref, zp_ref, s0_ref, hist_ref, onorm_ref,
                        pw_ref, ps_ref, mix_ref, sout_ref, st_scr, prev_scr, **static):
    for bb in range(MIX_BATCH):
        one = pl.ds(bb, 1)
        _mixer_kernel(q_ref.at[bb], k_ref.at[bb], lf_ref.at[bb], v_ref.at[bb], g_ref.at[bb], zp_ref.at[bb],
                      s0_ref.at[one], hist_ref.at[one], onorm_ref, pw_ref, ps_ref,
                      mix_ref.at[bb], sout_ref.at[one], st_scr.at[bb], prev_scr.at[bb], **static)


def _mixer(acts, s0, hist, out_norm, pool_w_bf, pool_scale, *, batch, t_len, c, pad, n_hist, skip):
    nc = t_len // c
    nb = MIX_BATCH
    assert nc * c == t_len and c % BASE == 0 and c >= HIST_ROWS and batch % nb == 0
    blk = lambda b, ci: (b, ci, 0)
    per_b4 = lambda b, ci: (b, 0, 0, 0)
    per_b3 = lambda b, ci: (b, 0, 0)
    c2 = lambda b, ci: (0, 0)
    c3 = lambda b, ci: (0, 0, 0)
    n_main = nc - skip
    mix_map = lambda b, ci: (b, jnp.maximum(ci - skip, 0), 0)
    kern = functools.partial(_mixer_group_kernel, c=c, pad=pad, n_hist=float(n_hist))
    mix, s_new = pl.pallas_call(
        kern,
        grid=(batch // nb, nc),
        in_specs=[pl.BlockSpec((nb, c, D_HGRN), blk)] * 6 + [
            pl.BlockSpec((nb, H_A, DK, DV), per_b4),
            pl.BlockSpec((nb, HIST_ROWS, D_POOL), per_b3),
            pl.BlockSpec((1, D_HGRN), c2),
            pl.BlockSpec((len(POOL_WINDOWS), POOL_GW, POOL_GW), c3),
            pl.BlockSpec((1, D_POOL), c2),
        ],
        out_specs=[
            pl.BlockSpec((nb, c, D_MODEL), mix_map),
            pl.BlockSpec((nb, H_A, DK, DV), per_b4),
        ],
        out_shape=[
            jax.ShapeDtypeStruct((batch, n_main * c, D_MODEL), BF16),
            jax.ShapeDtypeStruct((batch, H_A, DK, DV), F32),
        ],
        scratch_shapes=[
            pltpu.VMEM((nb, H_A, DV, DK), F32),
            pltpu.VMEM((nb, HIST_ROWS, D_POOL), F32),
        ],
        compiler_params=_cparams(("parallel", "arbitrary")),
        name="mixer",
    )(*[a.reshape(batch, t_len, D_HGRN) for a in acts], s0, hist, out_norm.reshape(1, D_HGRN), pool_w_bf,
      pool_scale.reshape(1, D_POOL))
    return mix.reshape(batch * n_main * c, D_MODEL), s_new


def _outproj_kernel(h_ref, mix_ref, wout_ref, nw_ref, h1_ref, xn_ref):
    h1 = h_ref[...] + jnp.dot(mix_ref[...], wout_ref[...], preferred_element_type=F32)
    h1_ref[...] = h1
    ms = jnp.mean(h1 * h1, axis=-1, keepdims=True)
    xn_ref[...] = h1 * lax.rsqrt(ms + EPS) * nw_ref[...]


def _outproj(h2d, mix, w_out_bf, norm_w):
    n = h2d.shape[0]
    tm = TM_PROJ
    assert n % tm == 0
    row = lambda i: (i, 0)
    const = lambda i: (0, 0)
    out = jax.ShapeDtypeStruct((n, D_MODEL), F32)
    return pl.pallas_call(
        _outproj_kernel,
        grid=(n // tm,),
        in_specs=[
            pl.BlockSpec((tm, D_MODEL), row),
            pl.BlockSpec((tm, D_MODEL), row),
            pl.BlockSpec((D_MODEL, D_MODEL), const),
            pl.BlockSpec((1, D_MODEL), const),
        ],
        out_specs=[pl.BlockSpec((tm, D_MODEL), row)] * 2,
        out_shape=[out, out],
        compiler_params=_cparams(("parallel",)),
        name="outproj",
    )(h2d, mix, w_out_bf, norm_w.reshape(1, D_MODEL))


def _keymat_kernel(sk_ref, wq_ref, m_ref):
    m = lax.dot_general(sk_ref[0], wq_ref[...], (((1,), (1,)), ((), ())),
                        precision=lax.Precision.HIGHEST, preferred_element_type=F32)
    m_ref[...] = m.astype(BF16)


def _keymat(subkeys, w_query):
    nhp = PEER_HEADS * 2
    sk = subkeys.reshape(nhp, N_KEYS, N_KEYS)
    return pl.pallas_call(
        _keymat_kernel,
        grid=(nhp,),
        in_specs=[
            pl.BlockSpec((1, N_KEYS, N_KEYS), lambda i: (i, 0, 0)),
            pl.BlockSpec((D_MODEL, N_KEYS), lambda i: (0, i)),
        ],
        out_specs=pl.BlockSpec((N_KEYS, D_MODEL), lambda i: (i, 0)),
        out_shape=jax.ShapeDtypeStruct((nhp * N_KEYS, D_MODEL), BF16),
        compiler_params=_cparams(("parallel",)),
        name="keymat",
    )(sk, w_query)


_STAIR = [(i, j) for i in range(PEER_TOPK) for j in range(PEER_TOPK) if (i + 1) * (j + 1) <= PEER_TOPK]


def _topk_kernel(xn_ref, m_ref, exp_ref, gate_ref, sc_scr, sv_scr, si_scr):
    tb = xn_ref.shape[0]
    xb = xn_ref[...].astype(BF16)
    sc_scr[...] = lax.dot_general(m_ref[...], xb, (((1,), (1,)), ((), ())), preferred_element_type=F32)
    key_i = lax.broadcasted_iota(jnp.int32, (N_KEYS, tb), 0).astype(F32)

    def head_body(h, carry):
        ss = [sc_scr[pl.ds(pl.multiple_of((2 * h + p) * N_KEYS, N_KEYS), N_KEYS), :] for p in range(2)]
        for i in range(PEER_TOPK):
            for p in range(2):
                s = ss[p]
                mx = jnp.max(s, axis=0, keepdims=True)
                idx = jnp.min(jnp.where(s == mx, key_i, float(N_KEYS)), axis=0, keepdims=True)
                ss[p] = jnp.where(key_i == idx, -jnp.inf, s)
                sv_scr[p, i, pl.ds(h, 1), :] = mx
                si_scr[p, i, pl.ds(h, 1), :] = idx
        return carry

    lax.fori_loop(0, PEER_HEADS, head_body, 0)

    sv0 = [sv_scr[0, i] for i in range(PEER_TOPK)]
    sv1 = [sv_scr[1, i] for i in range(PEER_TOPK)]
    si0 = [si_scr[0, i] for i in range(PEER_TOPK)]
    si1 = [si_scr[1, i] for i in range(PEER_TOPK)]
    cands = [sv0[i] + sv1[j] for (i, j) in _STAIR]
    flats = [float(i * PEER_TOPK + j) for (i, j) in _STAIR]
    tops, experts = [], []
    for r in range(PEER_TOPK):
        mx = functools.reduce(jnp.maximum, cands)
        sel = functools.reduce(jnp.minimum, [jnp.where(cv == mx, fl, 1e9) for cv, fl in zip(cands, flats)])
        cands = [jnp.where(sel == fl, -jnp.inf, cv) for cv, fl in zip(cands, flats)]
        fi = jnp.floor(sel * (1.0 / PEER_TOPK))
        fj = sel - fi * PEER_TOPK
        e0 = functools.reduce(jnp.add, [jnp.where(fi == float(i), si0[i], 0.0) for i in range(PEER_TOPK)])
        e1 = functools.reduce(jnp.add, [jnp.where(fj == float(j), si1[j], 0.0) for j in range(PEER_TOPK)])
        tops.append(mx)
        experts.append((e0 * N_KEYS + e1).astype(jnp.int32))
    ex = [jnp.exp(t - tops[0]) for t in tops]
    den = functools.reduce(jnp.add, ex)
    for r in range(PEER_TOPK):
        exp_ref[0, r] = experts[r]
        gate_ref[0, r] = ex[r] / den


def _topk(xn, keymat):
    n = xn.shape[0]
    tb = TB_TOPK
    assert n % tb == 0
    nb = n // tb
    shp = (nb, PEER_TOPK, PEER_HEADS, tb)
    ospec = pl.BlockSpec((1, PEER_TOPK, PEER_HEADS, tb), lambda i: (i, 0, 0, 0))
    return pl.pallas_call(
        _topk_kernel,
        grid=(nb,),
        in_specs=[
            pl.BlockSpec((tb, D_MODEL), lambda i: (i, 0)),
            pl.BlockSpec(keymat.shape, lambda i: (0, 0)),
        ],
        out_specs=[ospec, ospec],
        out_shape=[jax.ShapeDtypeStruct(shp, jnp.int32), jax.ShapeDtypeStruct(shp, F32)],
        scratch_shapes=[
            pltpu.VMEM((2 * PEER_HEADS * N_KEYS, tb), F32),
            pltpu.VMEM((2, PEER_TOPK, PEER_HEADS, tb), F32),
            pltpu.VMEM((2, PEER_TOPK, PEER_HEADS, tb), F32),
        ],
        compiler_params=_cparams(("parallel",)),
        name="topk",
    )(xn, keymat)


HALF_TILES = D_MODEL // 128


SUB = 8


def _as_tile(row):
    return jnp.concatenate([row[:, c * 128:(c + 1) * 128] for c in range(HALF_TILES)], axis=0)


def _as_row(tile):
    return jnp.concatenate([tile[c:c + 1, :] for c in range(HALF_TILES)], axis=1)


def _peer_kernel(first_ref, ahead_ref, gate_ref, xn_ref, h1_ref, nf_ref, gsum_ref, tab_ref, y_ref,
                 buf, w_scr, sem, *, n_groups):
    tb = xn_ref.shape[0]
    gps = tb // SUB
    step = pl.program_id(0)

    def start_row(row_ref, j, slot, e):
        pltpu.make_async_copy(tab_ref.at[row_ref[j, e]], buf.at[slot, e], sem.at[slot]).start(priority=e % 2)

    def gather_wait(slot):
        pltpu.make_async_copy(tab_ref.at[pl.ds(0, HITS)], buf.at[slot], sem.at[slot]).wait()

    @pl.when(step == 0)
    def _():
        for t in range(SUB):
            for e in range(HITS):
                start_row(first_ref, t, t, e)

    eye = (lax.broadcasted_iota(jnp.int32, (HITS, HITS), 0)
           == lax.broadcasted_iota(jnp.int32, (HITS, HITS), 1)).astype(F32)
    nf = _as_tile(nf_ref[...])
    gs = gsum_ref[...]

    def u_tile(slot, e):
        return buf[slot, e].astype(F32)[0:HALF_TILES, :]

    def v_tile(slot, e):
        return buf[slot, e].astype(F32)[HALF_TILES:2 * HALF_TILES, :]

    def v_row(v_slot, w_half, i, e):
        return w_scr[w_half, i, pl.ds(e, 1), :] * v_tile(v_slot + i, e)

    def finish(accs, h_row):
        y3 = _as_tile(h_row) + ((accs[0] + accs[1]) + (accs[2] + accs[3]))
        ms = jnp.sum(jnp.sum(y3 * y3, axis=1, keepdims=True), axis=0, keepdims=True) * (1.0 / D_MODEL)
        return _as_row(y3 * lax.rsqrt(ms + EPS) * nf)

    def group(g, with_prev):
        base = pl.multiple_of(g * SUB, SUB) if with_prev else 0
        gg = step * gps + g
        cur = lax.rem(gg, 3) * SUB
        nxt = lax.rem(gg + 1, 3) * SUB
        prv = lax.rem(gg + 2, 3) * SUB
        w_cur = lax.rem(gg, 2)
        x8 = xn_ref[pl.ds(base, SUB), :]
        g8 = gate_ref[pl.ds(base, SUB), :]
        if with_prev:
            hp8 = h1_ref[pl.ds(base - SUB, SUB), :]

        for i in range(SUB):
            gather_wait(cur + i)
        parts, rows = [], []
        for i in range(SUB):
            x3 = _as_tile(x8[i:i + 1, :])
            accs = [jnp.zeros((HALF_TILES, 128), F32) for _ in range(4)]
            chunks = []
            for k in range(HITS // CHUNK_HITS):
                prods = []
                for e in range(k * CHUNK_HITS, (k + 1) * CHUNK_HITS):
                    start_row(ahead_ref, base + i, nxt + i, e)
                    prods.append(u_tile(cur + i, e) * x3)
                    if with_prev:
                        accs[e % 4] = accs[e % 4] + v_row(prv, 1 - w_cur, i, e)
                pk = jnp.concatenate(prods, axis=0).astype(BF16)
                chunks.append(jnp.dot(gs, pk, preferred_element_type=F32))
            parts.append(jnp.concatenate(chunks, axis=0))
            if with_prev:
                rows.append(finish(accs, hp8[i:i + 1, :]))
        if with_prev:
            y_ref[pl.ds(base - SUB, SUB), :] = jnp.concatenate(rows, axis=0)

        for i in range(SUB):
            a_col = jnp.sum(parts[i], axis=1, keepdims=True)
            g_col = jnp.sum(eye * g8[i:i + 1, :], axis=1, keepdims=True)
            w_col = g_col * (0.5 * a_col * (1.0 + lax.erf(a_col * (2.0 ** -0.5))))
            w_scr[w_cur, i] = jnp.broadcast_to(w_col, (HITS, 128))

    group(0, False)

    def loop_body(g, carry):
        group(g, True)
        return carry

    lax.fori_loop(1, gps, loop_body, 0)

    gg = step * gps + (gps - 1)
    cur = lax.rem(gg, 3) * SUB
    w_cur = lax.rem(gg, 2)
    h8 = h1_ref[pl.ds(tb - SUB, SUB), :]
    rows = []
    for i in range(SUB):
        accs = [jnp.zeros((HALF_TILES, 128), F32) for _ in range(4)]
        for e in range(HITS):
            accs[e % 4] = accs[e % 4] + v_row(cur, w_cur, i, e)
        rows.append(finish(accs, h8[i:i + 1, :]))
    y_ref[pl.ds(tb - SUB, SUB), :] = jnp.concatenate(rows, axis=0)

    @pl.when(step == pl.num_programs(0) - 1)
    def _():
        for t in range(SUB):
            gather_wait((n_groups % 3) * SUB + t)


def _peer(idx, gates, xn, h1, norm_final, table):
    n = xn.shape[0]
    tb = TB_PEER
    assert n % tb == 0 and tb % SUB == 0 and tb // SUB >= 2
    assert AHEAD == SUB and N_SLOTS == 3 * SUB and HITS % CHUNK_HITS == 0
    r = jnp.arange(CHUNK_HITS * HALF_TILES, dtype=jnp.int32)[None, :] // HALF_TILES
    gsum = (r == jnp.arange(CHUNK_HITS, dtype=jnp.int32)[:, None]).astype(BF16)
    ahead = jnp.concatenate([idx[AHEAD:], jnp.broadcast_to(idx[n - 1:], (AHEAD, HITS))], axis=0)
    first = idx[:SUB]
    row = lambda i: (i, 0)
    const = lambda i: (0, 0)
    return pl.pallas_call(
        functools.partial(_peer_kernel, n_groups=n // SUB),
        grid=(n // tb,),
        in_specs=[
            pl.BlockSpec((SUB, HITS), const, memory_space=pltpu.SMEM),
            pl.BlockSpec((tb, HITS), row, memory_space=pltpu.SMEM),
            pl.BlockSpec((tb, HITS), row),
            pl.BlockSpec((tb, D_MODEL), row),
            pl.BlockSpec((tb, D_MODEL), row),
            pl.BlockSpec((1, D_MODEL), const),
            pl.BlockSpec(gsum.shape, const),
            pl.BlockSpec(memory_space=pl.ANY),
        ],
        out_specs=pl.BlockSpec((tb, D_MODEL), row),
        out_shape=jax.ShapeDtypeStruct((n, D_MODEL), F32),
        scratch_shapes=[
            pltpu.VMEM((N_SLOTS, HITS) + table.shape[1:], table.dtype),
            pltpu.VMEM((2, SUB, HITS, 128), F32),
            pltpu.SemaphoreType.DMA((N_SLOTS,)),
        ],
        compiler_params=_cparams(("arbitrary",)),
        name="peer",
    )(first, ahead, gates, xn, h1, norm_final.reshape(1, D_MODEL), gsum, table)


def _token_major(a):
    nb, r, h, t = a.shape
    return jnp.transpose(a, (0, 3, 2, 1)).reshape(nb * t, h * r)


def _stream(h_pad, h_main, s0, hist, n_hist, pad, skip, c, consts):
    (norm_mix, w_in_bf, lb_logits, out_norm, pool_w_bf, pool_scale, w_out_bf, norm_ffn,
     keymat, table, norm_final) = consts
    batch, t_len, _ = h_pad.shape
    acts = _inproj(h_pad.reshape(batch * t_len, D_MODEL), norm_mix, w_in_bf, lb_logits)
    mix, s_new = _mixer(acts, s0, hist, out_norm, pool_w_bf, pool_scale,
                        batch=batch, t_len=t_len, c=c, pad=pad, n_hist=n_hist, skip=skip)
    zp = acts[5].reshape(batch, t_len, D_POOL)
    new_hist = zp[:, t_len - POOL_HIST:, :]
    hm = h_main.reshape(-1, D_MODEL)
    h1, xn = _outproj(hm, mix, w_out_bf, norm_ffn)
    experts, gates = _topk(xn, keymat)
    y = _peer(_token_major(experts), _token_major(gates), xn, h1, norm_final, table)
    return y.reshape(h_main.shape), s_new, new_hist


def kernel(x_prompt, x_sample, state_hgrn, cache_pool, meta_tokens, norm_mix, w_in, hgrn_lb_logits,
           hgrn_out_norm, pool_w, pool_scale, w_out, norm_ffn, peer_w_query, peer_subkeys, peer_u,
           peer_v, norm_final):
    assert state_hgrn.shape[0] == 1, "single trunk layer"
    bp, seq, _ = x_prompt.shape
    bs, dseq, _ = x_sample.shape
    dt = x_prompt.dtype

    keymat = _keymat(peer_subkeys[0], peer_w_query[0])
    table = jnp.concatenate([peer_u[0], peer_v[0]], axis=1).astype(BF16).reshape(N_EXPERTS, 2 * HALF_TILES, 128)
    consts = (norm_mix[0], w_in[0].astype(BF16), hgrn_lb_logits, hgrn_out_norm[0],
              pool_w[0].astype(BF16), pool_scale[0], w_out[0].astype(BF16), norm_ffn[0],
              keymat, table, norm_final)

    pad = CHUNK - N_META
    head = jnp.concatenate([jnp.zeros((bp, pad, D_MODEL), dt),
                            jnp.broadcast_to(meta_tokens[None].astype(dt), (bp, N_META, D_MODEL))], axis=1)
    hp = jnp.concatenate([head, x_prompt], axis=1)
    s_p0 = jnp.zeros((bp, H_A, DK, DV), F32)
    hist_p0 = jnp.zeros((bp, HIST_ROWS, D_POOL), dt)
    y_p, s_p, c_p = _stream(hp, x_prompt, s_p0, hist_p0, 0, pad, 1, CHUNK, consts)

    hist_s0 = jnp.concatenate([jnp.zeros((bs, HIST_ROWS - POOL_HIST, D_POOL), dt), cache_pool[0]], axis=1)
    y_s, s_s, c_s = _stream(x_sample, x_sample, state_hgrn[0], hist_s0, POOL_HIST, 0, 0, dseq, consts)

    return (y_p, y_s, s_p[None].astype(dt), c_p[None], s_s[None].astype(dt), c_s[None])
```
